```python
import jax, jax.numpy as jnp
from jax import lax
import numpy as np

D_MODEL = 1024
BATCH = 8
SEQ = 2048
DEPTH = 4
DEC_BATCH = 32
DEC_SEQ = 1
PAST_LEN = 8192
PAGE_SIZE = 128

D_LRU = D_MODEL // 2
N_LRU_BLOCKS = 8
LRU_BLOCK = D_LRU // N_LRU_BLOCKS
CONV_W = 4
LRU_C = 8.0
N_HEADS_SB = 8
HEAD_DIM = 64
D_SB = N_HEADS_SB * HEAD_DIM
D_MIX = D_LRU + D_SB
D_IN = 2 * D_LRU + 4 * D_SB
Q_BLOCK = 128
RMS_EPS = 1e-6
SB_BIAS_HI = -5.0
SB_BIAS_LO = -9.0

kernel_name = 'hymba_rglru_stickbreaking_decode_step'


def rms_norm(x, g):
    x32 = x.astype(jnp.float32)
    y = x32 * lax.rsqrt(jnp.mean(x32 * x32, axis=-1, keepdims=True) + RMS_EPS)
    return (y * g.astype(jnp.float32)).astype(x.dtype)


def in_proj(xn, w):
    p = jnp.einsum('btd,de->bte', xn, w)
    B, T, _ = p.shape
    offs = np.cumsum([D_LRU, D_LRU, D_SB, D_SB, D_SB])
    u, g_lru, q, k, v, g_sb = jnp.split(p, list(offs), axis=-1)
    shp = (B, T, N_HEADS_SB, HEAD_DIM)
    return u, g_lru, q.reshape(shp), k.reshape(shp), v.reshape(shp), g_sb


def rglru_branch(u, conv_buf, h0, conv_w, conv_b, w_r, b_r, w_i, b_i, lam):
    B, T, _ = u.shape
    up = jnp.concatenate([conv_buf.astype(u.dtype), u], axis=1)
    xc = conv_b.astype(jnp.float32)
    for j in range(CONV_W):
        xc = xc + up[:, j:j + T].astype(jnp.float32) * conv_w[j].astype(jnp.float32)
    new_buf = up[:, up.shape[1] - (CONV_W - 1):]
    xb = xc.reshape(B, T, N_LRU_BLOCKS, LRU_BLOCK)
    r = jax.nn.sigmoid(jnp.einsum('btni,nij->btnj', xb, w_r.astype(jnp.float32)).reshape(B, T, D_LRU) + b_r)
    i = jax.nn.sigmoid(jnp.einsum('btni,nij->btnj', xb, w_i.astype(jnp.float32)).reshape(B, T, D_LRU) + b_i)
    log_a = LRU_C * r * jax.nn.log_sigmoid(lam.astype(jnp.float32))
    a = jnp.exp(log_a)
    b = jnp.sqrt(-jnp.expm1(2.0 * log_a)) * (i * xc)

    def combine(c1, c2):
        a1, b1 = c1
        a2, b2 = c2
        return a1 * a2, a2 * b1 + b2

    a_cum, b_cum = lax.associative_scan(combine, (a, b), axis=1)
    h = a_cum * h0.astype(jnp.float32)[:, None] + b_cum
    return h, h[:, -1], new_buf


def sb_attend(q, k, v, q_pos, k_pos, bias):
    z = jnp.einsum('bqhd,bkhd->bhqk', q, k, preferred_element_type=jnp.float32) * (HEAD_DIM ** -0.5)
    z = z + bias.astype(jnp.float32)[None, :, None, None]
    mask = k_pos[None, :] < q_pos[:, None]
    log_1m = jnp.where(mask, jax.nn.log_sigmoid(-z), 0.0)
    suffix = lax.cumsum(log_1m, axis=3, reverse=True) - log_1m
    w = jnp.where(mask, jnp.exp(jax.nn.log_sigmoid(z) + suffix), 0.0)
    o = jnp.einsum('bhqk,bkhd->bqhd', w, v.astype(jnp.float32))
    return o.astype(v.dtype)


def sb_prompt(q, k, v, bias):
    B, S, H, D = q.shape
    nb = S // Q_BLOCK
    pos = jnp.arange(S)
    qb = q.reshape(B, nb, Q_BLOCK, H, D).transpose(1, 0, 2, 3, 4)
    pb = pos.reshape(nb, Q_BLOCK)
    out = lax.map(lambda a: sb_attend(a[0], k, v, a[1], pos, bias), (qb, pb))
    return out.transpose(1, 0, 2, 3, 4).reshape(B, S, H, D)


def merge_out(h, g_lru, o, g_sb, w_out):
    B, T, _ = g_lru.shape
    y_lru = h.astype(g_lru.dtype) * jax.nn.silu(g_lru)
    y_sb = o.reshape(B, T, D_SB) * jax.nn.silu(g_sb)
    return jnp.einsum('bte,ed->btd', jnp.concatenate([y_lru, y_sb], axis=-1), w_out)


def setup_inputs(seed: int = 0) -> dict:
    key = jax.random.key(seed)
    ks = jax.random.split(key, 20)
    n_pages = PAST_LEN // PAGE_SIZE
    n_pool = (5 * DEC_BATCH * n_pages) // 4
    perm = jax.random.permutation(ks[0], n_pool)
    page_table = perm[:DEC_BATCH * n_pages].reshape(DEC_BATCH, n_pages).astype(jnp.int32)
    f32 = jnp.float32
    nrm = lambda k, s: jax.random.normal(k, s, f32)
    a0 = jax.random.uniform(ks[1], (DEPTH, D_LRU), f32, 0.9, 0.999)
    s0 = a0 ** (1.0 / LRU_C)
    lru_lambda = jnp.log(s0) - jnp.log1p(-s0)
    sb_bias = (jnp.linspace(SB_BIAS_HI, SB_BIAS_LO, N_HEADS_SB, dtype=f32)[None, :]
               + 0.1 * nrm(ks[18], (DEPTH, N_HEADS_SB)))
    return {
        'x_prompt': nrm(ks[2], (BATCH, SEQ, D_MODEL)),
        'x_sample': nrm(ks[3], (DEC_BATCH, DEC_SEQ, D_MODEL)),
        'cache_k': nrm(ks[4], (DEPTH, n_pool, PAGE_SIZE, N_HEADS_SB, HEAD_DIM)),
        'cache_v': nrm(ks[5], (DEPTH, n_pool, PAGE_SIZE, N_HEADS_SB, HEAD_DIM)),
        'state_lru_h': 0.5 * nrm(ks[6], (DEPTH, DEC_BATCH, D_LRU)),
        'state_conv': nrm(ks[7], (DEPTH, DEC_BATCH, CONV_W - 1, D_LRU)),
        'page_table': page_table,
        'norm_pre_g': 1.0 + 0.05 * nrm(ks[8], (DEPTH, D_MODEL)),
        'norm_post_g': 1.0 + 0.05 * nrm(ks[9], (DEPTH, D_MODEL)),
        'w_in': nrm(ks[10], (DEPTH, D_MODEL, D_IN)) * D_MODEL ** -0.5,
        'conv_w': nrm(ks[11], (DEPTH, CONV_W, D_LRU)) * CONV_W ** -0.5,
        'conv_b': 0.01 * nrm(ks[12], (DEPTH, D_LRU)),
        'w_rgate': nrm(ks[13], (DEPTH, N_LRU_BLOCKS, LRU_BLOCK, LRU_BLOCK)) * LRU_BLOCK ** -0.5,
        'b_rgate': 0.01 * nrm(ks[14], (DEPTH, D_LRU)),
        'w_igate': nrm(ks[15], (DEPTH, N_LRU_BLOCKS, LRU_BLOCK, LRU_BLOCK)) * LRU_BLOCK ** -0.5,
        'b_igate': 0.01 * nrm(ks[16], (DEPTH, D_LRU)),
        'lru_lambda': lru_lambda,
        'sb_bias': sb_bias,
        'w_out': nrm(ks[17], (DEPTH, D_MIX, D_MODEL)) * D_MIX ** -0.5,
    }


def reference(x_prompt, x_sample, cache_k, cache_v, state_lru_h, state_conv, page_table,
              norm_pre_g, norm_post_g, w_in, conv_w, conv_b, w_rgate, b_rgate,
              w_igate, b_igate, lru_lambda, sb_bias, w_out):
    B = x_prompt.shape[0]
    DB, DS, _ = x_sample.shape
    n_pages = page_table.shape[1]
    past_len = n_pages * cache_k.shape[2]
    q_pos_s = past_len + jnp.arange(DS)
    k_pos_s = jnp.arange(past_len + DS)
    xp, xs = x_prompt, x_sample
    kp_l, vp_l, hp_l, cp_l = [], [], [], []
    ks_l, vs_l, hs_l, cs_l = [], [], [], []
    for l in range(DEPTH):
        lru_p = (conv_w[l], conv_b[l], w_rgate[l], b_rgate[l], w_igate[l], b_igate[l], lru_lambda[l])
        u, g_lru, q, k, v, g_sb = in_proj(rms_norm(xp, norm_pre_g[l]), w_in[l])
        h, h_last, buf = rglru_branch(u, jnp.zeros((B, CONV_W - 1, D_LRU), u.dtype),
                                      jnp.zeros((B, D_LRU), jnp.float32), *lru_p)
        o = sb_prompt(q, k, v, sb_bias[l])
        xp = xp + rms_norm(merge_out(h, g_lru, o, g_sb, w_out[l]), norm_post_g[l])
        kp_l.append(k); vp_l.append(v); hp_l.append(h_last.astype(xp.dtype)); cp_l.append(buf)
        u, g_lru, q, k, v, g_sb = in_proj(rms_norm(xs, norm_pre_g[l]), w_in[l])
        h, h_last, buf = rglru_branch(u, state_conv[l], state_lru_h[l], *lru_p)
        k_past = cache_k[l][page_table].reshape(DB, past_len, N_HEADS_SB, HEAD_DIM)
        v_past = cache_v[l][page_table].reshape(DB, past_len, N_HEADS_SB, HEAD_DIM)
        k_all = jnp.concatenate([k_past.astype(k.dtype), k], axis=1)
        v_all = jnp.concatenate([v_past.astype(v.dtype), v], axis=1)
        o = sb_attend(q, k_all, v_all, q_pos_s, k_pos_s, sb_bias[l])
        xs = xs + rms_norm(merge_out(h, g_lru, o, g_sb, w_out[l]), norm_post_g[l])
        ks_l.append(k); vs_l.append(v); hs_l.append(h_last.astype(state_lru_h.dtype)); cs_l.append(buf)
    return (xp, xs,
            jnp.stack(kp_l), jnp.stack(vp_l), jnp.stack(hp_l), jnp.stack(cp_l),
            jnp.stack(ks_l), jnp.stack(vs_l), jnp.stack(hs_l), jnp.stack(cs_l))
```

```python
import functools

import jax
import jax.numpy as jnp
from jax import lax
from jax.experimental import pallas as pl
from jax.experimental.pallas import tpu as pltpu

F32 = jnp.float32
BF16 = jnp.bfloat16

RMS_EPS = 1e-6
LRU_C = 8.0
CONV_W = 4
N_HEADS = 8
HEAD_DIM = 64
D_HEADS = N_HEADS * HEAD_DIM

LANES = 128
SUBLANES = 8
VMEM_LIMIT_BYTES = 56 * 1024 * 1024

ROW_TILE = 512
LRU_CHUNK = 256
ATT_TILE = 256
PAGES_PER_STEP = 16
SPLIT_PARTS = 3


def _params(*semantics):
    return pltpu.CompilerParams(dimension_semantics=semantics, vmem_limit_bytes=VMEM_LIMIT_BYTES)


def _softplus(z):
    return jnp.maximum(z, 0.0) + jnp.log1p(jnp.exp(-jnp.abs(z)))


def _dot(a, b):
    return jnp.dot(a, b, preferred_element_type=F32)


def _dot_nt(a, b):
    return lax.dot_general(a, b, (((1,), (1,)), ((), ())), preferred_element_type=F32)


def _split_bf16(x, parts):
    pieces, r = [], x
    for _ in range(parts):
        p = r.astype(BF16)
        pieces.append(p)
        r = r - p.astype(F32)
    return pieces


def _dot_f32_by_01(x, w01, parts=SPLIT_PARTS):
    return sum(_dot(p, w01) for p in _split_bf16(x, parts))


def _dot_01_by_f32(w01, x, parts=SPLIT_PARTS):
    return sum(_dot(w01, p) for p in _split_bf16(x, parts))


def _inproj_kernel(x_ref, g_ref, *refs, n_plain, n_t):
    w_refs, out_refs = refs[:len(refs) - n_plain - n_t], refs[len(refs) - n_plain - n_t:]
    x = x_ref[...]
    y = x * lax.rsqrt(jnp.mean(x * x, axis=-1, keepdims=True) + RMS_EPS)
    xn = (y * g_ref[...]).astype(BF16)
    for i, o_ref in enumerate(out_refs[:n_plain]):
        width = o_ref.shape[-1]
        o_ref[...] = _dot(xn, w_refs[0][:, i * width:(i + 1) * width])
    for i, o_ref in enumerate(out_refs[n_plain:]):
        width = o_ref.shape[0]
        o_ref[...] = _dot_nt(w_refs[-1][pl.ds(i * width, width), :], xn)


def _inproj(x, g, w_plain, w_t, width):
    bsz, t, d = x.shape
    tm = min(ROW_TILE, t)
    assert t % tm == 0
    weights = [w_plain] + ([] if w_t is None else [w_t])
    n_plain, n_t = w_plain.shape[1] // width, 0 if w_t is None else w_t.shape[0] // width
    return pl.pallas_call(
        functools.partial(_inproj_kernel, n_plain=n_plain, n_t=n_t),
        grid=(bsz, t // tm),
        in_specs=[
            pl.BlockSpec((None, tm, d), lambda b, i: (b, i, 0)),
            pl.BlockSpec((1, d), lambda b, i: (0, 0)),
        ] + [pl.BlockSpec(w.shape, lambda b, i: (0, 0)) for w in weights],
        out_specs=[pl.BlockSpec((None, tm, width), lambda b, i: (b, i, 0))] * n_plain
        + [pl.BlockSpec((None, width, tm), lambda b, i: (b, 0, i))] * n_t,
        out_shape=[jax.ShapeDtypeStruct((bsz, t, width), F32)] * n_plain
        + [jax.ShapeDtypeStruct((bsz, width, t), F32)] * n_t,
        compiler_params=_params("parallel", "parallel"),
    )(x, g.reshape(1, d), *weights)


def _lru_gates(xc, wg_ref, br, bi, lam):
    half = xc.shape[-1] // 2
    xb = xc.astype(BF16)
    g0 = _dot(xb[:, :half], wg_ref[0])
    g1 = _dot(xb[:, half:], wg_ref[1])
    r = jax.nn.sigmoid(jnp.concatenate([g0[:, :half], g1[:, :half]], axis=-1) + br)
    i = jax.nn.sigmoid(jnp.concatenate([g0[:, half:], g1[:, half:]], axis=-1) + bi)
    log_a = LRU_C * r * (-_softplus(-lam))
    a = jnp.exp(log_a)
    b = jnp.sqrt(-jnp.tanh(log_a) * (a * a + 1.0)) * (i * xc)
    return a, b


def _lru_prompt_kernel(u_ref, conv0_ref, h0_ref, cw_ref, cb_ref, wg_ref, br_ref, bi_ref, lam_ref,
                       h_ref, hlast_ref, buf_ref, ext_ref, a_ref, b_ref, hc_ref):
    tc = u_ref.shape[0]
    pad = SUBLANES
    hist = CONV_W - 1

    @pl.when(pl.program_id(1) == 0)
    def _():
        hc_ref[...] = h0_ref[...]
        ext_ref[pl.ds(pad - hist, hist), :] = conv0_ref[...]

    ext_ref[pl.ds(pad, tc), :] = u_ref[...]
    xc = cb_ref[...] + ext_ref[pl.ds(pad - hist, tc), :] * cw_ref[pl.ds(0, 1), :]
    for j in range(1, CONV_W):
        xc = xc + ext_ref[pl.ds(pad - hist + j, tc), :] * cw_ref[pl.ds(j, 1), :]
    tail = ext_ref[pl.ds(pad + tc - hist, hist), :]
    ext_ref[pl.ds(pad - hist, hist), :] = tail
    buf_ref[...] = tail

    a, b = _lru_gates(xc, wg_ref, br_ref[...], bi_ref[...], lam_ref[...])
    a_ref[...] = a
    b_ref[...] = b

    def step(t, h):
        h = a_ref[pl.ds(t, 1), :] * h + b_ref[pl.ds(t, 1), :]
        h_ref[pl.ds(t, 1), :] = h
        return h

    h = lax.fori_loop(0, tc, step, hc_ref[...], unroll=8)
    hc_ref[...] = h
    hlast_ref[...] = h


def _lru_prompt(u, conv0, h0, cw, cb, wg, br, bi, lam):
    bsz, t, d = u.shape
    tc = min(LRU_CHUNK, t)
    hist = CONV_W - 1
    assert t % tc == 0 and tc >= hist
    vec = lambda: pl.BlockSpec((1, d), lambda b, i: (0, 0))
    return pl.pallas_call(
        _lru_prompt_kernel,
        grid=(bsz, t // tc),
        in_specs=[
            pl.BlockSpec((None, tc, d), lambda b, i: (b, i, 0)),
            pl.BlockSpec((None, hist, d), lambda b, i: (b, 0, 0)),
            pl.BlockSpec((None, 1, d), lambda b, i: (b, 0, 0)),
            pl.BlockSpec((CONV_W, d), lambda b, i: (0, 0)),
            vec(),
            pl.BlockSpec(wg.shape, lambda b, i: (0, 0, 0)),
            vec(), vec(), vec(),
        ],
        out_specs=[
            pl.BlockSpec((None, tc, d), lambda b, i: (b, i, 0)),
            pl.BlockSpec((None, 1, d), lambda b, i: (b, 0, 0)),
            pl.BlockSpec((None, hist, d), lambda b, i: (b, 0, 0)),
        ],
        out_shape=[
            jax.ShapeDtypeStruct((bsz, t, d), F32),
            jax.ShapeDtypeStruct((bsz, 1, d), F32),
            jax.ShapeDtypeStruct((bsz, hist, d), F32),
        ],
        scratch_shapes=[
            pltpu.VMEM((tc + SUBLANES, d), F32),
            pltpu.VMEM((tc, d), F32),
            pltpu.VMEM((tc, d), F32),
            pltpu.VMEM((1, d), F32),
        ],
        compiler_params=_params("parallel", "arbitrary"),
    )(u, conv0, h0.reshape(bsz, 1, d), cw, cb.reshape(1, d), wg, br.reshape(1, d), bi.reshape(1, d),
      lam.reshape(1, d))


def _lru_step_kernel(u_ref, conv_ref, h0_ref, cw_ref, cb_ref, wg_ref, br_ref, bi_ref, lam_ref,
                     h_ref, buf_ref):
    u = u_ref[...]
    xc = cb_ref[...] + u * cw_ref[pl.ds(CONV_W - 1, 1), :]
    for j in range(CONV_W - 1):
        xc = xc + conv_ref[j] * cw_ref[pl.ds(j, 1), :]
    a, b = _lru_gates(xc, wg_ref, br_ref[...], bi_ref[...], lam_ref[...])
    h_ref[...] = a * h0_ref[...] + b
    for j in range(CONV_W - 2):
        buf_ref[j] = conv_ref[j + 1]
    buf_ref[CONV_W - 2] = u


def _lru_step(u, conv_t, h0, cw, cb, wg, br, bi, lam):
    bsz, d = u.shape
    return pl.pallas_call(
        _lru_step_kernel,
        out_shape=[jax.ShapeDtypeStruct((bsz, d), F32), jax.ShapeDtypeStruct(conv_t.shape, F32)],
        compiler_params=pltpu.CompilerParams(vmem_limit_bytes=VMEM_LIMIT_BYTES),
    )(u, conv_t, h0, cw, cb.reshape(1, d), wg, br.reshape(1, d), bi.reshape(1, d), lam.reshape(1, d))


def _sb_prompt_kernel(bias_ref, q_ref, kt_ref, vt_ref, o_ref):
    tq = q_ref.shape[0]
    tk = tq
    qi = pl.program_id(1)
    scale = HEAD_DIM ** -0.5
    row = lax.broadcasted_iota(jnp.int32, (tq, tk), 0)
    col = lax.broadcasted_iota(jnp.int32, (tq, tk), 1)
    tri = (lax.broadcasted_iota(jnp.int32, (tk, tk), 0) >= lax.broadcasted_iota(jnp.int32, (tk, tk), 1)
           ).astype(BF16)

    for h in range(N_HEADS):
        feats = pl.ds(h * HEAD_DIM, HEAD_DIM)
        qh = q_ref[:, feats].astype(BF16)
        bias = bias_ref[h]

        def tile(step, state):
            acc, carry = state
            j = qi - step
            keys = pl.ds(pl.multiple_of(j * tk, tk), tk)
            kt = kt_ref[feats, keys].astype(BF16)
            vt = vt_ref[feats, keys].astype(BF16)
            z = _dot(qh, kt) * scale + bias
            mask = (col + j * tk) < (row + qi * tq)
            m = jnp.where(mask, -_softplus(z), 0.0)
            csum = _dot_f32_by_01(m, tri, parts=2)
            w = jnp.where(mask, jnp.exp(z + csum + carry), 0.0)
            acc = acc + _dot_nt(w.astype(BF16), vt)
            return acc, carry + csum[:, 0:1]

        acc, _ = lax.fori_loop(0, qi + 1, tile,
                               (jnp.zeros((tq, HEAD_DIM), F32), jnp.zeros((tq, 1), F32)))
        o_ref[:, feats] = acc


def _sb_prompt(q, kt, vt, bias):
    bsz, t, d = q.shape
    tq = min(ATT_TILE, t)
    assert t % tq == 0 and d == D_HEADS
    return pl.pallas_call(
        _sb_prompt_kernel,
        grid=(bsz, t // tq),
        in_specs=[
            pl.BlockSpec(memory_space=pltpu.SMEM),
            pl.BlockSpec((None, tq, d), lambda b, i: (b, i, 0)),
            pl.BlockSpec((None, d, t), lambda b, i: (b, 0, 0)),
            pl.BlockSpec((None, d, t), lambda b, i: (b, 0, 0)),
        ],
        out_specs=pl.BlockSpec((None, tq, d), lambda b, i: (b, i, 0)),
        out_shape=jax.ShapeDtypeStruct((bsz, t, d), F32),
        compiler_params=_params("parallel", "arbitrary"),
    )(bias, q, kt, vt)


def _sb_paged_kernel(pt_ref, q_ref, knew_ref, vnew_ref, bias_ref, *refs, pages_per_step, past_len):
    del pt_ref
    pp = pages_per_step
    k_refs, v_refs = refs[:pp], refs[pp:2 * pp]
    o_ref, qcol_ref, acc_ref, carry_ref = refs[2 * pp:]
    j = pl.program_id(1)
    d, page = k_refs[0].shape
    rows = pp * N_HEADS
    scale = HEAD_DIM ** -0.5

    @pl.when(j == 0)
    def _():
        q = q_ref[...]
        qcol_ref[...] = jnp.transpose(jnp.broadcast_to(q, (page, d)))
        assert knew_ref.shape[0] == 1
        new_idx = jnp.zeros((N_HEADS, page), jnp.int32)
        mask = (past_len + new_idx) < (past_len + new_idx)
        own = (lax.broadcasted_iota(jnp.int32, (N_HEADS, d), 0)
               == lax.broadcasted_iota(jnp.int32, (N_HEADS, d), 1) // HEAD_DIM)
        z = jnp.sum(jnp.where(own, q * knew_ref[...], 0.0), axis=-1, keepdims=True) * scale
        z = z + bias_ref[0:N_HEADS, :]
        carry_ref[...] = jnp.where(mask, -_softplus(z), 0.0)
        w = jnp.where(mask, jnp.exp(z), 0.0)
        vnew_col = jnp.transpose(jnp.broadcast_to(vnew_ref[...], (page, d)))
        acc_ref[...] = (vnew_col.reshape(N_HEADS, HEAD_DIM, page) * w[:, None, :]).reshape(d, page)

    qcol = qcol_ref[...].reshape(N_HEADS, HEAD_DIM, page)
    z = jnp.concatenate(
        [jnp.sum(k_refs[i][...].reshape(N_HEADS, HEAD_DIM, page) * qcol, axis=1) for i in range(pp)], axis=0)
    z = z * scale + bias_ref[...]
    m = -_softplus(z)
    ki = lax.broadcasted_iota(jnp.int32, (page, page), 0)
    si = lax.broadcasted_iota(jnp.int32, (page, page), 1)
    tri = (ki >= si).astype(BF16)
    ones = jnp.ones((page, page), BF16)
    ri = lax.broadcasted_iota(jnp.int32, (rows, rows), 0)
    ci = lax.broadcasted_iota(jnp.int32, (rows, rows), 1)
    later = ((ri % N_HEADS == ci % N_HEADS) & (ci // N_HEADS > ri // N_HEADS)).astype(BF16)
    csum = _dot_f32_by_01(m, tri)
    total = _dot_f32_by_01(m, ones)
    after = _dot_01_by_f32(later, total)
    carry = carry_ref[...]
    w = jnp.exp(z + csum + after + jnp.concatenate([carry] * pp, axis=0))
    acc = acc_ref[...].reshape(N_HEADS, HEAD_DIM, page)
    for i in range(pp):
        acc = acc + v_refs[i][...].reshape(N_HEADS, HEAD_DIM, page) * w[i * N_HEADS:(i + 1) * N_HEADS, None, :]
    acc_ref[...] = acc.reshape(d, page)
    carry_ref[...] = carry + after[0:N_HEADS, :] + total[0:N_HEADS, :]

    @pl.when(j == pl.num_programs(1) - 1)
    def _():
        o_ref[...] = jnp.sum(jnp.transpose(acc_ref[...]), axis=0, keepdims=True)


def _sb_paged(q, k_new, v_new, cache_kt, cache_vt, layer, page_table, bias):
    bsz, d = q.shape
    n_pages = page_table.shape[1]
    page = cache_kt.shape[3]
    assert d == D_HEADS and page == LANES
    pp = min(PAGES_PER_STEP, n_pages)
    assert n_pages % pp == 0
    steps = n_pages // pp
    bias_tile = jnp.tile(bias.reshape(N_HEADS, 1), (pp, page))

    def page_spec(i):
        return pl.BlockSpec((None, None, d, page),
                            lambda b, j, pt: (layer, pt[b, n_pages - pp * (j + 1) + i], 0, 0))

    seq = lambda: pl.BlockSpec((None, 1, d), lambda b, j, pt: (b, 0, 0))
    kernel = functools.partial(_sb_paged_kernel, pages_per_step=pp, past_len=n_pages * page)
    out = pl.pallas_call(
        kernel,
        grid_spec=pltpu.PrefetchScalarGridSpec(
            num_scalar_prefetch=1,
            grid=(bsz, steps),
            in_specs=[seq(), seq(), seq(), pl.BlockSpec(bias_tile.shape, lambda b, j, pt: (0, 0))]
            + [page_spec(i) for i in range(pp)] * 2,
            out_specs=seq(),
            scratch_shapes=[pltpu.VMEM((d, page), F32), pltpu.VMEM((d, page), F32),
                            pltpu.VMEM((N_HEADS, page), F32)],
        ),
        out_shape=jax.ShapeDtypeStruct((bsz, 1, d), F32),
        compiler_params=_params("parallel", "arbitrary"),
    )(page_table, q.reshape(bsz, 1, d), k_new.reshape(bsz, 1, d), v_new.reshape(bsz, 1, d), bias_tile,
      *([cache_kt] * pp), *([cache_vt] * pp))
    return out.reshape(bsz, d)


def _outproj_kernel(x_ref, h_ref, gl_ref, o_ref, gs_ref, w_ref, g_ref, y_ref):
    d_lru = h_ref.shape[-1]
    y_lru = (h_ref[...] * jax.nn.silu(gl_ref[...])).astype(BF16)
    y_sb = (o_ref[...] * jax.nn.silu(gs_ref[...])).astype(BF16)
    m = _dot(y_lru, w_ref[pl.ds(0, d_lru), :]) + _dot(y_sb, w_ref[pl.ds(d_lru, w_ref.shape[0] - d_lru), :])
    n = m * lax.rsqrt(jnp.mean(m * m, axis=-1, keepdims=True) + RMS_EPS)
    y_ref[...] = x_ref[...] + n * g_ref[...]


def _outproj(x2d, h, g_lru, o, g_sb, w_bf16, g):
    n, d = x2d.shape
    tm = min(ROW_TILE, n)
    assert n % tm == 0
    rows = lambda w: pl.BlockSpec((tm, w), lambda i: (i, 0))
    return pl.pallas_call(
        _outproj_kernel,
        grid=(n // tm,),
        in_specs=[rows(d), rows(h.shape[1]), rows(g_lru.shape[1]), rows(o.shape[1]), rows(g_sb.shape[1]),
                  pl.BlockSpec(w_bf16.shape, lambda i: (0, 0)),
                  pl.BlockSpec((1, d), lambda i: (0, 0))],
        out_specs=rows(d),
        out_shape=jax.ShapeDtypeStruct((n, d), F32),
        compiler_params=_params("parallel"),
    )(x2d, h, g_lru, o, g_sb, w_bf16, g.reshape(1, d))


def _gate_weights(w_r, w_i):
    nb, blk, _ = w_r.shape
    half_blocks = nb // 2

    def halves(w):
        eye = jnp.eye(half_blocks, dtype=w.dtype)
        w = w.reshape(2, half_blocks, blk, blk)
        return jnp.einsum('cnij,nm->cnimj', w, eye).reshape(2, half_blocks * blk, half_blocks * blk)

    return jnp.concatenate([halves(w_r), halves(w_i)], axis=-1).astype(BF16)


def _feature_major(a):
    lead = a.shape[:-3]
    n = len(lead)
    return jnp.transpose(a, (*range(n), n + 1, n + 2, n)).reshape(*lead, a.shape[-2] * a.shape[-1], a.shape[-3])


def _position_major(a, heads):
    lead = a.shape[:-2]
    n = len(lead)
    a = a.reshape(*lead, heads, a.shape[-2] // heads, a.shape[-1])
    return jnp.transpose(a, (*range(n), n + 2, n, n + 1))


def kernel(x_prompt, x_sample, cache_k, cache_v, state_lru_h, state_conv, page_table, norm_pre_g, norm_post_g, w_in, conv_w, conv_b, w_rgate, b_rgate, w_igate, b_igate, lru_lambda, sb_bias, w_out):
    bsz, seq, d_model = x_prompt.shape
    dec_b, dec_s, _ = x_sample.shape
    depth = w_in.shape[0]
    d_lru = lru_lambda.shape[1]
    assert dec_s == 1 and d_lru == D_HEADS
    width = d_lru
    ckt = _feature_major(cache_k)
    cvt = _feature_major(cache_v)
    col = lambda i: slice(i * width, (i + 1) * width)

    xp = x_prompt
    xs = x_sample.reshape(1, dec_b * dec_s, d_model)
    zero_conv = jnp.zeros((bsz, CONV_W - 1, d_lru), F32)
    zero_h = jnp.zeros((bsz, d_lru), F32)
    kp_l, vp_l, hp_l, cp_l, ks_l, vs_l, hs_l, cs_l = ([] for _ in range(8))
    for l in range(depth):
        w_l = w_in[l].astype(BF16)
        w_ugqg = jnp.concatenate([w_l[:, col(0)], w_l[:, col(1)], w_l[:, col(2)], w_l[:, col(5)]], axis=1)
        w_kv_t = jnp.transpose(w_l[:, 3 * width:5 * width])
        w_out_l = w_out[l].astype(BF16)
        wg = _gate_weights(w_rgate[l], w_igate[l])
        lru_p = (conv_w[l], conv_b[l], wg, b_rgate[l], b_igate[l], lru_lambda[l])

        u, g_lru, q, g_sb, kt, vt = _inproj(xp, norm_pre_g[l], w_ugqg, w_kv_t, width)
        h, h_last, buf = _lru_prompt(u, zero_conv, zero_h, *lru_p)
        o = _sb_prompt(q, kt, vt, sb_bias[l])
        flat = lambda a: a.reshape(bsz * seq, a.shape[-1])
        xp = _outproj(flat(xp), flat(h), flat(g_lru), flat(o), flat(g_sb), w_out_l,
                      norm_post_g[l]).reshape(bsz, seq, d_model)
        kp_l.append(kt)
        vp_l.append(vt)
        hp_l.append(h_last.reshape(bsz, d_lru))
        cp_l.append(buf)

        u, g_lru, q, k, v, g_sb = (a[0] for a in _inproj(xs, norm_pre_g[l], w_l, None, width))
        h, buf_t = _lru_step(u, jnp.swapaxes(state_conv[l], 0, 1), state_lru_h[l], *lru_p)
        o = _sb_paged(q, k, v, ckt, cvt, l, page_table, sb_bias[l])
        xs = _outproj(xs[0], h, g_lru, o, g_sb, w_out_l, norm_post_g[l])[None]
        ks_l.append(k.reshape(dec_b, dec_s, N_HEADS, HEAD_DIM))
        vs_l.append(v.reshape(dec_b, dec_s, N_HEADS, HEAD_DIM))
        hs_l.append(h)
        cs_l.append(buf_t)

    return (xp, xs.reshape(dec_b, dec_s, d_model),
            _position_major(jnp.stack(kp_l), N_HEADS), _position_major(jnp.stack(vp_l), N_HEADS),
            jnp.stack(hp_l), jnp.stack(cp_l),
            jnp.stack(ks_l), jnp.stack(vs_l), jnp.stack(hs_l), jnp.swapaxes(jnp.stack(cs_l), 1, 2))
```

```python
import functools

import jax
import jax.numpy as jnp
from jax import lax
from jax.experimental import pallas as pl
from jax.experimental.pallas import tpu as pltpu

F32 = jnp.float32
BF16 = jnp.bfloat16

RMS_EPS = 1e-6
LRU_C = 8.0
CONV_W = 4
N_HEADS = 8
HEAD_DIM = 64
D_HEADS = N_HEADS * HEAD_DIM

LANES = 128
SUBLANES = 8
VMEM_LIMIT_BYTES = 56 * 1024 * 1024

ROW_TILE = 512
LRU_CHUNK = 256
ATT_TILE = 256
PAGES_PER_STEP = 16
SPLIT_PARTS = 3


def _params(*semantics):
    return pltpu.CompilerParams(dimension_semantics=semantics, vmem_limit_bytes=VMEM_LIMIT_BYTES)


def _softplus(z):
    return jnp.maximum(z, 0.0) + jnp.log1p(jnp.exp(-jnp.abs(z)))


LOG2_E = 1.4426950408889634


def _log2_one_minus_sigmoid(z2):
    n2 = -z2
    return jnp.minimum(n2, 0.0) - jnp.log2(1.0 + jnp.exp2(jnp.minimum(z2, n2)))


def _dot(a, b):
    return jnp.dot(a, b, preferred_element_type=F32)


def _dot_nt(a, b):
    return lax.dot_general(a, b, (((1,), (1,)), ((), ())), preferred_element_type=F32)


def _split_bf16(x, parts):
    pieces, r = [], x
    for _ in range(parts):
        p = r.astype(BF16)
        pieces.append(p)
        r = r - p.astype(F32)
    return pieces


def _dot_f32_by_01(x, w01, parts=SPLIT_PARTS):
    return sum(_dot(p, w01) for p in _split_bf16(x, parts))


def _dot_01_by_f32(w01, x, parts=SPLIT_PARTS):
    return sum(_dot(w01, p) for p in _split_bf16(x, parts))


def _inproj_kernel(x_ref, g_ref, *refs, n_plain, n_t):
    w_refs, out_refs = refs[:len(refs) - n_plain - n_t], refs[len(refs) - n_plain - n_t:]
    x = x_ref[...]
    y = x * lax.rsqrt(jnp.mean(x * x, axis=-1, keepdims=True) + RMS_EPS)
    xn = (y * g_ref[...]).astype(BF16)
    for i, o_ref in enumerate(out_refs[:n_plain]):
        width = o_ref.shape[-1]
        o_ref[...] = _dot(xn, w_refs[0][:, i * width:(i + 1) * width])
    for i, o_ref in enumerate(out_refs[n_plain:]):
        width = o_ref.shape[0]
        o_ref[...] = _dot_nt(w_refs[-1][pl.ds(i * width, width), :], xn)


def _inproj(x, g, w_plain, w_t, width):
    bsz, t, d = x.shape
    tm = min(ROW_TILE, t)
    assert t % tm == 0
    weights = [w_plain] + ([] if w_t is None else [w_t])
    n_plain, n_t = w_plain.shape[1] // width, 0 if w_t is None else w_t.shape[0] // width
    return pl.pallas_call(
        functools.partial(_inproj_kernel, n_plain=n_plain, n_t=n_t),
        grid=(bsz, t // tm),
        in_specs=[
            pl.BlockSpec((None, tm, d), lambda b, i: (b, i, 0)),
            pl.BlockSpec((1, d), lambda b, i: (0, 0)),
        ] + [pl.BlockSpec(w.shape, lambda b, i: (0, 0)) for w in weights],
        out_specs=[pl.BlockSpec((None, tm, width), lambda b, i: (b, i, 0))] * n_plain
        + [pl.BlockSpec((None, width, tm), lambda b, i: (b, 0, i))] * n_t,
        out_shape=[jax.ShapeDtypeStruct((bsz, t, width), F32)] * n_plain
        + [jax.ShapeDtypeStruct((bsz, width, t), F32)] * n_t,
        compiler_params=_params("parallel", "parallel"),
    )(x, g.reshape(1, d), *weights)


def _lru_gates(xc, wg_ref, br, bi, lam):
    half = xc.shape[-1] // 2
    xb = xc.astype(BF16)
    g0 = _dot(xb[:, :half], wg_ref[0])
    g1 = _dot(xb[:, half:], wg_ref[1])
    r = jax.nn.sigmoid(jnp.concatenate([g0[:, :half], g1[:, :half]], axis=-1) + br)
    i = jax.nn.sigmoid(jnp.concatenate([g0[:, half:], g1[:, half:]], axis=-1) + bi)
    log_a = LRU_C * r * (-_softplus(-lam))
    a = jnp.exp(log_a)
    b = jnp.sqrt(-jnp.tanh(log_a) * (a * a + 1.0)) * (i * xc)
    return a, b


def _lru_prompt_kernel(u_ref, conv0_ref, h0_ref, cw_ref, cb_ref, wg_ref, br_ref, bi_ref, lam_ref,
                       h_ref, hlast_ref, buf_ref, ext_ref, a_ref, b_ref, hc_ref):
    tc = u_ref.shape[0]
    pad = SUBLANES
    hist = CONV_W - 1

    @pl.when(pl.program_id(1) == 0)
    def _():
        hc_ref[...] = h0_ref[...]
        ext_ref[pl.ds(pad - hist, hist), :] = conv0_ref[...]

    ext_ref[pl.ds(pad, tc), :] = u_ref[...]
    xc = cb_ref[...] + ext_ref[pl.ds(pad - hist, tc), :] * cw_ref[pl.ds(0, 1), :]
    for j in range(1, CONV_W):
        xc = xc + ext_ref[pl.ds(pad - hist + j, tc), :] * cw_ref[pl.ds(j, 1), :]
    tail = ext_ref[pl.ds(pad + tc - hist, hist), :]
    ext_ref[pl.ds(pad - hist, hist), :] = tail
    buf_ref[...] = tail

    a, b = _lru_gates(xc, wg_ref, br_ref[...], bi_ref[...], lam_ref[...])

    groups, d = tc // SUBLANES, a.shape[-1]
    ga = a.reshape(groups, SUBLANES, d)
    gb = b.reshape(groups, SUBLANES, d)
    step_in_group = lax.broadcasted_iota(jnp.int32, ga.shape, 1)
    for k in (1 << e for e in range(SUBLANES.bit_length() - 1)):
        has_prev = step_in_group >= k
        gb = jnp.where(has_prev, ga * pltpu.roll(gb, k, axis=1) + gb, gb)
        ga = jnp.where(has_prev, ga * pltpu.roll(ga, k, axis=1), ga)
    a_ref[...] = ga.reshape(tc, d)
    b_ref[...] = gb.reshape(tc, d)

    def group(i, h):
        rows = pl.ds(pl.multiple_of(i * SUBLANES, SUBLANES), SUBLANES)
        hg = a_ref[rows, :] * h + b_ref[rows, :]
        h_ref[rows, :] = hg
        return hg[SUBLANES - 1:SUBLANES, :]

    h = lax.fori_loop(0, groups, group, hc_ref[...], unroll=4)
    hc_ref[...] = h
    hlast_ref[...] = h


def _lru_prompt(u, conv0, h0, cw, cb, wg, br, bi, lam):
    bsz, t, d = u.shape
    tc = min(LRU_CHUNK, t)
    hist = CONV_W - 1
    assert t % tc == 0 and tc >= hist and tc % SUBLANES == 0
    vec = lambda: pl.BlockSpec((1, d), lambda b, i: (0, 0))
    return pl.pallas_call(
        _lru_prompt_kernel,
        grid=(bsz, t // tc),
        in_specs=[
            pl.BlockSpec((None, tc, d), lambda b, i: (b, i, 0)),
            pl.BlockSpec((None, hist, d), lambda b, i: (b, 0, 0)),
            pl.BlockSpec((None, 1, d), lambda b, i: (b, 0, 0)),
            pl.BlockSpec((CONV_W, d), lambda b, i: (0, 0)),
            vec(),
            pl.BlockSpec(wg.shape, lambda b, i: (0, 0, 0)),
            vec(), vec(), vec(),
        ],
        out_specs=[
            pl.BlockSpec((None, tc, d), lambda b, i: (b, i, 0)),
            pl.BlockSpec((None, 1, d), lambda b, i: (b, 0, 0)),
            pl.BlockSpec((None, hist, d), lambda b, i: (b, 0, 0)),
        ],
        out_shape=[
            jax.ShapeDtypeStruct((bsz, t, d), F32),
            jax.ShapeDtypeStruct((bsz, 1, d), F32),
            jax.ShapeDtypeStruct((bsz, hist, d), F32),
        ],
        scratch_shapes=[
            pltpu.VMEM((tc + SUBLANES, d), F32),
            pltpu.VMEM((tc, d), F32),
            pltpu.VMEM((tc, d), F32),
            pltpu.VMEM((1, d), F32),
        ],
        compiler_params=_params("parallel", "arbitrary"),
    )(u, conv0, h0.reshape(bsz, 1, d), cw, cb.reshape(1, d), wg, br.reshape(1, d), bi.reshape(1, d),
      lam.reshape(1, d))


def _lru_step_kernel(u_ref, conv_ref, h0_ref, cw_ref, cb_ref, wg_ref, br_ref, bi_ref, lam_ref,
                     h_ref, buf_ref):
    u = u_ref[...]
    xc = cb_ref[...] + u * cw_ref[pl.ds(CONV_W - 1, 1), :]
    for j in range(CONV_W - 1):
        xc = xc + conv_ref[j] * cw_ref[pl.ds(j, 1), :]
    a, b = _lru_gates(xc, wg_ref, br_ref[...], bi_ref[...], lam_ref[...])
    h_ref[...] = a * h0_ref[...] + b
    for j in range(CONV_W - 2):
        buf_ref[j] = conv_ref[j + 1]
    buf_ref[CONV_W - 2] = u


def _lru_step(u, conv_t, h0, cw, cb, wg, br, bi, lam):
    bsz, d = u.shape
    return pl.pallas_call(
        _lru_step_kernel,
        out_shape=[jax.ShapeDtypeStruct((bsz, d), F32), jax.ShapeDtypeStruct(conv_t.shape, F32)],
        compiler_params=pltpu.CompilerParams(vmem_limit_bytes=VMEM_LIMIT_BYTES),
    )(u, conv_t, h0, cw, cb.reshape(1, d), wg, br.reshape(1, d), bi.reshape(1, d), lam.reshape(1, d))


def _sb_prompt_kernel(bias_ref, tri_ref, q_ref, kt_ref, vt_ref, o_ref, qm_ref, carry_ref):
    tq = q_ref.shape[0]
    tk = tri_ref.shape[0]
    qi = pl.program_id(1)
    pair = 2 * HEAD_DIM
    scale = HEAD_DIM ** -0.5

    lane_head = lax.broadcasted_iota(jnp.int32, (tq, pair), 1) // HEAD_DIM
    for h in range(N_HEADS):
        qp = q_ref[:, pl.ds((h // 2) * pair, pair)] * scale
        qm_ref[h] = jnp.where(lane_head == h % 2, qp, 0.0).astype(BF16)
    row_head = lax.broadcasted_iota(jnp.int32, (pair, tk), 0) // HEAD_DIM
    visible = (lax.broadcasted_iota(jnp.int32, (tq, tk), 1) < lax.broadcasted_iota(jnp.int32, (tq, tk), 0))

    def visit(j, diagonal):
        keys = pl.ds(pl.multiple_of(j * tk, tk), tk)
        for p in range(N_HEADS // 2):
            feats = pl.ds(p * pair, pair)
            ktp = kt_ref[feats, keys].astype(BF16)
            vtp = vt_ref[feats, keys].astype(BF16)
            out = None
            for s in range(2):
                h = 2 * p + s
                z = (_dot(qm_ref[h], ktp) + bias_ref[h]) * LOG2_E
                m = _log2_one_minus_sigmoid(z)
                if diagonal:
                    m = jnp.where(visible, m, 0.0)
                csum = _dot(m.astype(BF16), tri_ref[...])
                logw = z + csum if diagonal else z + csum + carry_ref[h]
                w = jnp.exp2(logw)
                if diagonal:
                    w = jnp.where(visible, w, 0.0)
                pv = _dot_nt(w.astype(BF16), jnp.where(row_head == s, vtp, jnp.zeros_like(vtp)))
                out = pv if out is None else out + pv
                total = csum[:, 0:1]
                carry_ref[h] = total if diagonal else carry_ref[h] + total
            if diagonal:
                o_ref[:, feats] = out
            else:
                o_ref[:, feats] += out

    visit(qi, True)

    def earlier(step, c):
        visit(qi - 1 - step, False)
        return c

    lax.fori_loop(0, qi, earlier, 0)


def _sb_prompt(q, kt, vt, bias):
    bsz, t, d = q.shape
    tq = min(ATT_TILE, t)
    assert t % tq == 0 and d == D_HEADS
    tri = (jnp.arange(tq)[:, None] >= jnp.arange(tq)[None, :]).astype(BF16)
    return pl.pallas_call(
        _sb_prompt_kernel,
        grid=(bsz, t // tq),
        in_specs=[
            pl.BlockSpec(memory_space=pltpu.SMEM),
            pl.BlockSpec((tq, tq), lambda b, i: (0, 0)),
            pl.BlockSpec((None, tq, d), lambda b, i: (b, i, 0)),
            pl.BlockSpec((None, d, t), lambda b, i: (b, 0, 0)),
            pl.BlockSpec((None, d, t), lambda b, i: (b, 0, 0)),
        ],
        out_specs=pl.BlockSpec((None, tq, d), lambda b, i: (b, i, 0)),
        out_shape=jax.ShapeDtypeStruct((bsz, t, d), F32),
        scratch_shapes=[pltpu.VMEM((N_HEADS, tq, 2 * HEAD_DIM), BF16), pltpu.VMEM((N_HEADS, tq, 1), F32)],
        compiler_params=_params("parallel", "arbitrary"),
    )(bias, tri, q, kt, vt)


def _sb_paged_kernel(pt_ref, q_ref, knew_ref, vnew_ref, bias_ref, *refs, pages_per_step, past_len):
    del pt_ref
    pp = pages_per_step
    k_refs, v_refs = refs[:pp], refs[pp:2 * pp]
    o_ref, qcol_ref, acc_ref, carry_ref = refs[2 * pp:]
    j = pl.program_id(1)
    d, page = k_refs[0].shape
    rows = pp * N_HEADS
    scale = HEAD_DIM ** -0.5

    @pl.when(j == 0)
    def _():
        q = q_ref[...]
        qcol_ref[...] = jnp.transpose(jnp.broadcast_to(q, (page, d)))
        assert knew_ref.shape[0] == 1
        new_idx = jnp.zeros((N_HEADS, page), jnp.int32)
        mask = (past_len + new_idx) < (past_len + new_idx)
        own = (lax.broadcasted_iota(jnp.int32, (N_HEADS, d), 0)
               == lax.broadcasted_iota(jnp.int32, (N_HEADS, d), 1) // HEAD_DIM)
        z = jnp.sum(jnp.where(own, q * knew_ref[...], 0.0), axis=-1, keepdims=True) * scale
        z = (z + bias_ref[0:N_HEADS, :]) * LOG2_E
        carry_ref[...] = jnp.where(mask, _log2_one_minus_sigmoid(z), 0.0)
        w = jnp.where(mask, jnp.exp2(z), 0.0)
        vnew_col = jnp.transpose(jnp.broadcast_to(vnew_ref[...], (page, d)))
        acc_ref[...] = (vnew_col.reshape(N_HEADS, HEAD_DIM, page) * w[:, None, :]).reshape(d, page)

    qcol = qcol_ref[...].reshape(N_HEADS, HEAD_DIM, page)
    z = jnp.concatenate(
        [jnp.sum(k_refs[i][...].reshape(N_HEADS, HEAD_DIM, page) * qcol, axis=1) for i in range(pp)], axis=0)
    z = (z * scale + bias_ref[...]) * LOG2_E
    m = _log2_one_minus_sigmoid(z)
    ki = lax.broadcasted_iota(jnp.int32, (page, page), 0)
    si = lax.broadcasted_iota(jnp.int32, (page, page), 1)
    tri = (ki >= si).astype(BF16)
    ri =lax.broadcasted_iota(jnp.int32, (rows, rows), 0)
    ci = lax.broadcasted_iota(jnp.int32, (rows, rows), 1)
    later = ((ri % N_HEADS == ci % N_HEADS) & (ci // N_HEADS > ri // N_HEADS)).astype(BF16)
    csum = _dot_f32_by_01(m, tri)
    total = jnp.broadcast_to(csum[:, 0:1], csum.shape)
    after = _dot_01_by_f32(later, total)
    carry = carry_ref[...]
    w = jnp.exp2(z + csum + after + jnp.concatenate([carry] * pp, axis=0))
    acc = acc_ref[...].reshape(N_HEADS, HEAD_DIM, page)
    for i in range(pp):
        acc = acc + v_refs[i][...].reshape(N_HEADS, HEAD_DIM, page) * w[i * N_HEADS:(i + 1) * N_HEADS, None, :]
    acc_ref[...] = acc.reshape(d, page)
    carry_ref[...] = carry + after[0:N_HEADS, :] + total[0:N_HEADS, :]

    @pl.when(j == pl.num_programs(1) - 1)
    def _():
        o_ref[...] = jnp.sum(jnp.transpose(acc_ref[...]), axis=0, keepdims=True)


def _sb_paged(q, k_new, v_new, cache_kt, cache_vt, layer, page_table, bias):
    bsz, d = q.shape
    n_pages = page_table.shape[1]
    page = cache_kt.shape[3]
    assert d == D_HEADS and page == LANES
    pp = min(PAGES_PER_STEP, n_pages)
    assert n_pages % pp == 0
    steps = n_pages // pp
    bias_tile = jnp.tile(bias.reshape(N_HEADS, 1), (pp, page))

    def page_spec(i):
        return pl.BlockSpec((None, None, d, page),
                            lambda b, j, pt: (layer, pt[b, n_pages - pp * (j + 1) + i], 0, 0))

    seq = lambda: pl.BlockSpec((None, 1, d), lambda b, j, pt: (b, 0, 0))
    kernel = functools.partial(_sb_paged_kernel, pages_per_step=pp, past_len=n_pages * page)
    out = pl.pallas_call(
        kernel,
        grid_spec=pltpu.PrefetchScalarGridSpec(
            num_scalar_prefetch=1,
            grid=(bsz, steps),
            in_specs=[seq(), seq(), seq(), pl.BlockSpec(bias_tile.shape, lambda b, j, pt: (0, 0))]
            + [page_spec(i) for i in range(pp)] * 2,
            out_specs=seq(),
            scratch_shapes=[pltpu.VMEM((d, page), F32), pltpu.VMEM((d, page), F32),
                            pltpu.VMEM((N_HEADS, page), F32)],
        ),
        out_shape=jax.ShapeDtypeStruct((bsz, 1, d), F32),
        compiler_params=_params("parallel", "arbitrary"),
    )(page_table, q.reshape(bsz, 1, d), k_new.reshape(bsz, 1, d), v_new.reshape(bsz, 1, d), bias_tile,
      *([cache_kt] * pp), *([cache_vt] * pp))
    return out.reshape(bsz, d)


def _outproj_kernel(x_ref, h_ref, gl_ref, o_ref, gs_ref, w_ref, g_ref, y_ref):
    d_lru = h_ref.shape[-1]
    y_lru = (h_ref[...] * jax.nn.silu(gl_ref[...])).astype(BF16)
    y_sb = (o_ref[...] * jax.nn.silu(gs_ref[...])).astype(BF16)
    m = _dot(y_lru, w_ref[pl.ds(0, d_lru), :]) + _dot(y_sb, w_ref[pl.ds(d_lru, w_ref.shape[0] - d_lru), :])
    n = m * lax.rsqrt(jnp.mean(m * m, axis=-1, keepdims=True) + RMS_EPS)
    y_ref[...] = x_ref[...] + n * g_ref[...]


def _outproj(x2d, h, g_lru, o, g_sb, w_bf16, g):
    n, d = x2d.shape
    tm = min(ROW_TILE, n)
    assert n % tm == 0
    rows = lambda w: pl.BlockSpec((tm, w), lambda i: (i, 0))
    return pl.pallas_call(
        _outproj_kernel,
        grid=(n // tm,),
        in_specs=[rows(d), rows(h.shape[1]), rows(g_lru.shape[1]), rows(o.shape[1]), rows(g_sb.shape[1]),
                  pl.BlockSpec(w_bf16.shape, lambda i: (0, 0)),
                  pl.BlockSpec((1, d), lambda i: (0, 0))],
        out_specs=rows(d),
        out_shape=jax.ShapeDtypeStruct((n, d), F32),
        compiler_params=_params("parallel"),
    )(x2d, h, g_lru, o, g_sb, w_bf16, g.reshape(1, d))


def _gate_weights(w_r, w_i):
    nb, blk, _ = w_r.shape
    half_blocks = nb // 2

    def halves(w):
        eye = jnp.eye(half_blocks, dtype=w.dtype)
        w = w.reshape(2, half_blocks, blk, blk)
        return jnp.einsum('cnij,nm->cnimj', w, eye).reshape(2, half_blocks * blk, half_blocks * blk)

    return jnp.concatenate([halves(w_r), halves(w_i)], axis=-1).astype(BF16)


def _feature_major(a):
    lead = a.shape[:-3]
    n = len(lead)
    return jnp.transpose(a, (*range(n), n + 1, n + 2, n)).reshape(*lead, a.shape[-2] * a.shape[-1], a.shape[-3])


def _position_major(a, heads):
    lead = a.shape[:-2]
    n = len(lead)
    a = a.reshape(*lead, heads, a.shape[-2] // heads, a.shape[-1])
    return jnp.transpose(a, (*range(n), n + 2, n, n + 1))


def kernel(x_prompt, x_sample, cache_k, cache_v, state_lru_h, state_conv, page_table, norm_pre_g, norm_post_g, w_in, conv_w, conv_b, w_rgate, b_rgate, w_igate, b_igate, lru_lambda, sb_bias, w_out):
    bsz, seq, d_model = x_prompt.shape
    dec_b, dec_s, _ = x_sample.shape
    depth = w_in.shape[0]
    d_lru = lru_lambda.shape[1]
    assert dec_s == 1 and d_lru == D_HEADS
    width = d_lru
    ckt = _feature_major(cache_k)
    cvt = _feature_major(cache_v)
    col = lambda i: slice(i * width, (i + 1) * width)

    xp = x_prompt
    xs = x_sample.reshape(1, dec_b * dec_s, d_model)
    zero_conv = jnp.zeros((bsz, CONV_W - 1, d_lru), F32)
    zero_h = jnp.zeros((bsz, d_lru), F32)
    kp_l, vp_l, hp_l, cp_l, ks_l, vs_l, hs_l, cs_l = ([] for _ in range(8))
    for l in range(depth):
        w_l = w_in[l].astype(BF16)
        w_ugqg = jnp.concatenate([w_l[:, col(0)], w_l[:, col(1)], w_l[:, col(2)], w_l[:, col(5)]], axis=1)
        w_kv_t = jnp.transpose(w_l[:, 3 * width:5 * width])
        w_out_l = w_out[l].astype(BF16)
        wg = _gate_weights(w_rgate[l], w_igate[l])
        lru_p = (conv_w[l], conv_b[l], wg, b_rgate[l], b_igate[l], lru_lambda[l])

        u, g_lru, q, g_sb, kt, vt = _inproj(xp, norm_pre_g[l], w_ugqg, w_kv_t, width)
        h, h_last, buf = _lru_prompt(u, zero_conv, zero_h, *lru_p)
        o = _sb_prompt(q, kt, vt, sb_bias[l])
        flat = lambda a: a.reshape(bsz * seq, a.shape[-1])
        xp = _outproj(flat(xp), flat(h), flat(g_lru), flat(o), flat(g_sb), w_out_l,
                      norm_post_g[l]).reshape(bsz, seq, d_model)
        kp_l.append(kt)
        vp_l.append(vt)
        hp_l.append(h_last.reshape(bsz, d_lru))
        cp_l.append(buf)

        u, g_lru, q, k, v, g_sb = (a[0] for a in _inproj(xs, norm_pre_g[l], w_l, None, width))
        h, buf_t = _lru_step(u, jnp.swapaxes(state_conv[l], 0, 1), state_lru_h[l], *lru_p)
        o = _sb_paged(q, k, v, ckt, cvt, l, page_table, sb_bias[l])
        xs = _outproj(xs[0], h, g_lru, o, g_sb, w_out_l, norm_post_g[l])[None]
        ks_l.append(k.reshape(dec_b, dec_s, N_HEADS, HEAD_DIM))
        vs_l.append(v.reshape(dec_b, dec_s, N_HEADS, HEAD_DIM))
        hs_l.append(h)
        cs_l.append(buf_t)

    return (xp, xs.reshape(dec_b, dec_s, d_model),
            _position_major(jnp.stack(kp_l), N_HEADS), _position_major(jnp.stack(vp_l), N_HEADS),
            jnp.stack(hp_l), jnp.stack(cp_l),
            jnp.stack(ks_l), jnp.stack(vs_l), jnp.stack(hs_l), jnp.swapaxes(jnp.stack(cs_l), 1, 2))
```

```python
import functools

import jax
import jax.numpy as jnp
from jax import lax
from jax.experimental import pallas as pl
from jax.experimental.pallas import tpu as pltpu

F32 = jnp.float32
BF16 = jnp.bfloat16

RMS_EPS = 1e-6
LRU_C = 8.0
CONV_W = 4
N_HEADS = 8
HEAD_DIM = 64
D_HEADS = N_HEADS * HEAD_DIM

LANES = 128
SUBLANES = 8
VMEM_LIMIT_BYTES = 56 * 1024 * 1024

ROW_TILE = 1024
LRU_CHUNK = 256
ATT_TILE = 256
PAGES_PER_STEP = 32
SPLIT_PARTS = 3


def _params(*semantics):
    return pltpu.CompilerParams(dimension_semantics=semantics, vmem_limit_bytes=VMEM_LIMIT_BYTES)


def _softplus(z):
    return jnp.maximum(z, 0.0) + jnp.log1p(jnp.exp(-jnp.abs(z)))


LOG2_E = 1.4426950408889634


def _log2_one_minus_sigmoid(z2):
    n2 = -z2
    return jnp.minimum(n2, 0.0) - jnp.log2(1.0 + jnp.exp2(jnp.minimum(z2, n2)))


def _dot(a, b):
    return jnp.dot(a, b, preferred_element_type=F32)


def _dot_nt(a, b):
    return lax.dot_general(a, b, (((1,), (1,)), ((), ())), preferred_element_type=F32)


def _split_bf16(x, parts):
    pieces, r = [], x
    for _ in range(parts):
        p = r.astype(BF16)
        pieces.append(p)
        r = r - p.astype(F32)
    return pieces


def _dot_f32_by_01(x, w01, parts=SPLIT_PARTS):
    return sum(_dot(p, w01) for p in _split_bf16(x, parts))


def _dot_01_by_f32(w01, x, parts=SPLIT_PARTS):
    return sum(_dot(w01, p) for p in _split_bf16(x, parts))


def _inproj_kernel(x_ref, g_ref, *refs, n_plain, n_t):
    w_refs, out_refs = refs[:len(refs) - n_plain - n_t], refs[len(refs) - n_plain - n_t:]
    x = x_ref[...]
    y = x * lax.rsqrt(jnp.mean(x * x, axis=-1, keepdims=True) + RMS_EPS)
    xn = (y * g_ref[...]).astype(BF16)
    for i, o_ref in enumerate(out_refs[:n_plain]):
        width = o_ref.shape[-1]
        o_ref[...] = _dot(xn, w_refs[0][:, i * width:(i + 1) * width])
    for i, o_ref in enumerate(out_refs[n_plain:]):
        width = o_ref.shape[0]
        o_ref[...] = _dot_nt(w_refs[-1][pl.ds(i * width, width), :], xn)


def _inproj(x, g, w_plain, w_t, width):
    bsz, t, d = x.shape
    tm = min(ROW_TILE, t)
    assert t % tm == 0
    weights = [w_plain] + ([] if w_t is None else [w_t])
    n_plain, n_t = w_plain.shape[1] // width, 0 if w_t is None else w_t.shape[0] // width
    return pl.pallas_call(
        functools.partial(_inproj_kernel, n_plain=n_plain, n_t=n_t),
        grid=(bsz, t // tm),
        in_specs=[
            pl.BlockSpec((None, tm, d), lambda b, i: (b, i, 0)),
            pl.BlockSpec((1, d), lambda b, i: (0, 0)),
        ] + [pl.BlockSpec(w.shape, lambda b, i: (0, 0)) for w in weights],
        out_specs=[pl.BlockSpec((None, tm, width), lambda b, i: (b, i, 0))] * n_plain
        + [pl.BlockSpec((None, width, tm), lambda b, i: (b, 0, i))] * n_t,
        out_shape=[jax.ShapeDtypeStruct((bsz, t, width), F32)] * n_plain
        + [jax.ShapeDtypeStruct((bsz, width, t), F32)] * n_t,
        compiler_params=_params("parallel", "parallel"),
    )(x, g.reshape(1, d), *weights)


def _lru_gates(xc, wg_ref, br, bi, lam):
    half = xc.shape[-1] // 2
    xb = xc.astype(BF16)
    g0 = _dot(xb[:, :half], wg_ref[0])
    g1 = _dot(xb[:, half:], wg_ref[1])
    r = jax.nn.sigmoid(jnp.concatenate([g0[:, :half], g1[:, :half]], axis=-1) + br)
    i = jax.nn.sigmoid(jnp.concatenate([g0[:, half:], g1[:, half:]], axis=-1) + bi)
    log_a = LRU_C * r * (-_softplus(-lam))
    a = jnp.exp(log_a)
    b = jnp.sqrt(-jnp.tanh(log_a) * (a * a + 1.0)) * (i * xc)
    return a, b


def _lru_prompt_kernel(u_ref, conv0_ref, h0_ref, cw_ref, cb_ref, wg_ref, br_ref, bi_ref, lam_ref,
                       h_ref, hlast_ref, buf_ref, ext_ref, a_ref, b_ref, hc_ref):
    tc = u_ref.shape[0]
    pad = SUBLANES
    hist = CONV_W - 1

    @pl.when(pl.program_id(1) == 0)
    def _():
        hc_ref[...] = h0_ref[...]
        ext_ref[pl.ds(pad - hist, hist), :] = conv0_ref[...]

    ext_ref[pl.ds(pad, tc), :] = u_ref[...]
    xc = cb_ref[...] + ext_ref[pl.ds(pad - hist, tc), :] * cw_ref[pl.ds(0, 1), :]
    for j in range(1, CONV_W):
        xc = xc + ext_ref[pl.ds(pad - hist + j, tc), :] * cw_ref[pl.ds(j, 1), :]
    tail = ext_ref[pl.ds(pad + tc - hist, hist), :]
    ext_ref[pl.ds(pad - hist, hist), :] = tail
    buf_ref[...] = tail

    a, b = _lru_gates(xc, wg_ref, br_ref[...], bi_ref[...], lam_ref[...])

    groups, d = tc // SUBLANES, a.shape[-1]
    ga = a.reshape(groups, SUBLANES, d)
    gb = b.reshape(groups, SUBLANES, d)
    step_in_group = lax.broadcasted_iota(jnp.int32, ga.shape, 1)
    for k in (1 << e for e in range(SUBLANES.bit_length() - 1)):
        has_prev = step_in_group >= k
        gb = jnp.where(has_prev, ga * pltpu.roll(gb, k, axis=1) + gb, gb)
        ga = jnp.where(has_prev, ga * pltpu.roll(ga, k, axis=1), ga)
    a_ref[...] = ga.reshape(tc, d)
    b_ref[...] = gb.reshape(tc, d)

    def group(i, h):
        rows = pl.ds(pl.multiple_of(i * SUBLANES, SUBLANES), SUBLANES)
        hg = a_ref[rows, :] * h + b_ref[rows, :]
        h_ref[rows, :] = hg
        return hg[SUBLANES - 1:SUBLANES, :]

    h = lax.fori_loop(0, groups, group, hc_ref[...], unroll=4)
    hc_ref[...] = h
    hlast_ref[...] = h


def _lru_prompt(u, conv0, h0, cw, cb, wg, br, bi, lam):
    bsz, t, d = u.shape
    tc = min(LRU_CHUNK, t)
    hist = CONV_W - 1
    assert t % tc == 0 and tc >= hist and tc % SUBLANES == 0
    vec = lambda: pl.BlockSpec((1, d), lambda b, i: (0, 0))
    return pl.pallas_call(
        _lru_prompt_kernel,
        grid=(bsz, t // tc),
        in_specs=[
            pl.BlockSpec((None, tc, d), lambda b, i: (b, i, 0)),
            pl.BlockSpec((None, hist, d), lambda b, i: (b, 0, 0)),
            pl.BlockSpec((None, 1, d), lambda b, i: (b, 0, 0)),
            pl.BlockSpec((CONV_W, d), lambda b, i: (0, 0)),
            vec(),
            pl.BlockSpec(wg.shape, lambda b, i: (0, 0, 0)),
            vec(), vec(), vec(),
        ],
        out_specs=[
            pl.BlockSpec((None, tc, d), lambda b, i: (b, i, 0)),
            pl.BlockSpec((None, 1, d), lambda b, i: (b, 0, 0)),
            pl.BlockSpec((None, hist, d), lambda b, i: (b, 0, 0)),
        ],
        out_shape=[
            jax.ShapeDtypeStruct((bsz, t, d), F32),
            jax.ShapeDtypeStruct((bsz, 1, d), F32),
            jax.ShapeDtypeStruct((bsz, hist, d), F32),
        ],
        scratch_shapes=[
            pltpu.VMEM((tc + SUBLANES, d), F32),
            pltpu.VMEM((tc, d), F32),
            pltpu.VMEM((tc, d), F32),
            pltpu.VMEM((1, d), F32),
        ],
        compiler_params=_params("parallel", "arbitrary"),
    )(u, conv0, h0.reshape(bsz, 1, d), cw, cb.reshape(1, d), wg, br.reshape(1, d), bi.reshape(1, d),
      lam.reshape(1, d))


def _lru_step_kernel(u_ref, conv_ref, h0_ref, cw_ref, cb_ref, wg_ref, br_ref, bi_ref, lam_ref,
                     h_ref, buf_ref):
    u = u_ref[...]
    xc = cb_ref[...] + u * cw_ref[pl.ds(CONV_W - 1, 1), :]
    for j in range(CONV_W - 1):
        xc = xc + conv_ref[j] * cw_ref[pl.ds(j, 1), :]
    a, b = _lru_gates(xc, wg_ref, br_ref[...], bi_ref[...], lam_ref[...])
    h_ref[...] = a * h0_ref[...] + b
    for j in range(CONV_W - 2):
        buf_ref[j] = conv_ref[j + 1]
    buf_ref[CONV_W - 2] = u


def _lru_step(u, conv_t, h0, cw, cb, wg, br, bi, lam):
    bsz, d = u.shape
    return pl.pallas_call(
        _lru_step_kernel,
        out_shape=[jax.ShapeDtypeStruct((bsz, d), F32), jax.ShapeDtypeStruct(conv_t.shape, F32)],
        compiler_params=pltpu.CompilerParams(vmem_limit_bytes=VMEM_LIMIT_BYTES),
    )(u, conv_t, h0, cw, cb.reshape(1, d), wg, br.reshape(1, d), bi.reshape(1, d), lam.reshape(1, d))


def _sb_prompt_kernel(bias_ref, tri_ref, q_ref, kt_ref, vt_ref, o_ref, qm_ref, pre_ref, total_ref, carry_ref):
    tq = q_ref.shape[0]
    tk = tri_ref.shape[0]
    qi = pl.program_id(1)
    pair = 2 * HEAD_DIM
    scale = HEAD_DIM ** -0.5

    lane_head = lax.broadcasted_iota(jnp.int32, (tq, pair), 1) // HEAD_DIM
    for h in range(N_HEADS):
        qp = q_ref[:, pl.ds((h // 2) * pair, pair)] * scale
        qm_ref[h] = jnp.where(lane_head == h % 2, qp, 0.0).astype(BF16)
    row_head = lax.broadcasted_iota(jnp.int32, (pair, tk), 0) // HEAD_DIM
    visible = (lax.broadcasted_iota(jnp.int32, (tq, tk), 1) < lax.broadcasted_iota(jnp.int32, (tq, tk), 0))

    def keys_of(j):
        return pl.ds(pl.multiple_of(j * tk, tk), tk)

    def front(j, diagonal):
        for p in range(N_HEADS // 2):
            ktp = kt_ref[pl.ds(p * pair, pair), keys_of(j)].astype(BF16)
            for s in range(2):
                h = 2 * p + s
                z = (_dot(qm_ref[h], ktp) + bias_ref[h]) * LOG2_E
                m = _log2_one_minus_sigmoid(z)
                if diagonal:
                    m = jnp.where(visible, m, 0.0)
                csum = _dot(m.astype(BF16), tri_ref[...])
                pre = z + csum
                pre_ref[h] = jnp.where(visible, pre, -jnp.inf) if diagonal else pre
                total_ref[h] = csum[:, 0:1]

    def back(j):
        for p in range(N_HEADS // 2):
            feats = pl.ds(p * pair, pair)
            vtp = vt_ref[feats, keys_of(j)].astype(BF16)
            out = None
            for s in range(2):
                h = 2 * p + s
                carry = carry_ref[h]
                w = jnp.exp2(pre_ref[h] + carry)
                pv = _dot_nt(w.astype(BF16), jnp.where(row_head == s, vtp, jnp.zeros_like(vtp)))
                out = pv if out is None else out + pv
                carry_ref[h] = carry + total_ref[h]
            o_ref[:, feats] += out

    o_ref[...] = jnp.zeros(o_ref.shape, F32)
    carry_ref[...] = jnp.zeros(carry_ref.shape, F32)
    front(qi, True)

    def trip(i, c):
        back(qi - i + 1)
        front(qi - i, False)
        return c

    lax.fori_loop(1, qi + 1, trip, 0)
    back(0)


def _sb_prompt(q, kt, vt, bias):
    bsz, t, d = q.shape
    tq = min(ATT_TILE, t)
    assert t % tq == 0 and d == D_HEADS
    tri = (jnp.arange(tq)[:, None] >= jnp.arange(tq)[None, :]).astype(BF16)
    return pl.pallas_call(
        _sb_prompt_kernel,
        grid=(bsz, t // tq),
        in_specs=[
            pl.BlockSpec(memory_space=pltpu.SMEM),
            pl.BlockSpec((tq, tq), lambda b, i: (0, 0)),
            pl.BlockSpec((None, tq, d), lambda b, i: (b, i, 0)),
            pl.BlockSpec((None, d, t), lambda b, i: (b, 0, 0)),
            pl.BlockSpec((None, d, t), lambda b, i: (b, 0, 0)),
        ],
        out_specs=pl.BlockSpec((None, tq, d), lambda b, i: (b, i, 0)),
        out_shape=jax.ShapeDtypeStruct((bsz, t, d), F32),
        scratch_shapes=[pltpu.VMEM((N_HEADS, tq, 2 * HEAD_DIM), BF16), pltpu.VMEM((N_HEADS, tq, tq), F32),
                        pltpu.VMEM((N_HEADS, tq, 1), F32), pltpu.VMEM((N_HEADS, tq, 1), F32)],
        compiler_params=_params("parallel", "arbitrary"),
    )(bias, tri, q, kt, vt)


def _sb_paged_kernel(pt_ref, q_ref, knew_ref, vnew_ref, bias_ref, *refs, pages_per_step, past_len):
    del pt_ref
    pp = pages_per_step
    k_refs, v_refs = refs[:pp], refs[pp:2 * pp]
    o_ref, qcol_ref, acc_ref, carry_ref, w_ref = refs[2 * pp:]
    j = pl.program_id(1)
    d, page = k_refs[0].shape
    rows = pp * N_HEADS
    scale = HEAD_DIM ** -0.5

    @pl.when(j == 0)
    def _():
        q = q_ref[...]
        qcol_ref[...] = jnp.transpose(jnp.broadcast_to(q, (page, d)))
        assert knew_ref.shape[0] == 1
        new_idx = jnp.zeros((N_HEADS, page), jnp.int32)
        mask = (past_len + new_idx) < (past_len + new_idx)
        own = (lax.broadcasted_iota(jnp.int32, (N_HEADS, d), 0)
               == lax.broadcasted_iota(jnp.int32, (N_HEADS, d), 1) // HEAD_DIM)
        z = jnp.sum(jnp.where(own, q * knew_ref[...], 0.0), axis=-1, keepdims=True) * scale
        head_bias = jnp.concatenate([bias_ref[pl.ds(h * pp, 1), :] for h in range(N_HEADS)], axis=0)
        z = (z + head_bias) * LOG2_E
        carry_ref[...] = jnp.where(mask, _log2_one_minus_sigmoid(z), 0.0)
        w = jnp.where(mask, jnp.exp2(z), 0.0)
        vnew_col = jnp.transpose(jnp.broadcast_to(vnew_ref[...], (page, d)))
        acc_ref[...] = (vnew_col.reshape(N_HEADS, HEAD_DIM, page) * w[:, None, :]).reshape(d, page)

    z_rows = []
    for h in range(N_HEADS):
        feats = pl.ds(h * HEAD_DIM, HEAD_DIM)
        qh = qcol_ref[feats, :]
        z_rows += [jnp.sum(k_refs[i][feats, :] * qh, axis=0, keepdims=True) for i in range(pp)]
    z = jnp.concatenate(z_rows, axis=0)
    z = (z * scale + bias_ref[...]) * LOG2_E
    m = _log2_one_minus_sigmoid(z)
    ki = lax.broadcasted_iota(jnp.int32, (page, page), 0)
    si = lax.broadcasted_iota(jnp.int32, (page, page), 1)
    tri = (ki >= si).astype(BF16)
    ri = lax.broadcasted_iota(jnp.int32, (rows, rows), 0)
    ci = lax.broadcasted_iota(jnp.int32, (rows, rows), 1)
    later = ((ri // pp == ci // pp) & (ci > ri)).astype(BF16)
    csum = _dot_f32_by_01(m, tri)
    total = jnp.broadcast_to(csum[:, 0:1], csum.shape)
    after = _dot_01_by_f32(later, total)
    carry = carry_ref[...]
    carry_rows = jnp.concatenate([jnp.broadcast_to(carry[h:h + 1, :], (pp, page)) for h in range(N_HEADS)], axis=0)
    w_ref[...] = jnp.exp2(z + csum + after + carry_rows)
    for h in range(N_HEADS):
        feats = pl.ds(h * HEAD_DIM, HEAD_DIM)
        acc = acc_ref[feats, :]
        for i in range(pp):
            acc = acc + v_refs[i][feats, :] * w_ref[pl.ds(h * pp + i, 1), :]
        acc_ref[feats, :] = acc
    first = after + total
    carry_ref[...] = carry + jnp.concatenate([first[h * pp:h * pp + 1, :] for h in range(N_HEADS)], axis=0)

    @pl.when(j == pl.num_programs(1) - 1)
    def _():
        o_ref[...] = jnp.sum(jnp.transpose(acc_ref[...]), axis=0, keepdims=True)


def _sb_paged(q, k_new, v_new, cache_kt, cache_vt, layer, page_table, bias):
    bsz, d = q.shape
    n_pages = page_table.shape[1]
    page = cache_kt.shape[3]
    assert d == D_HEADS and page == LANES
    pp = min(PAGES_PER_STEP, n_pages)
    assert n_pages % pp == 0
    steps = n_pages // pp
    bias_tile = jnp.broadcast_to(bias.reshape(N_HEADS, 1, 1), (N_HEADS, pp, page)).reshape(N_HEADS * pp, page)

    def page_spec(i):
        return pl.BlockSpec((None, None, d, page),
                            lambda b, j, pt: (layer, pt[b, n_pages - pp * (j + 1) + i], 0, 0))

    seq = lambda: pl.BlockSpec((None, 1, d), lambda b, j, pt: (b, 0, 0))
    kernel = functools.partial(_sb_paged_kernel, pages_per_step=pp, past_len=n_pages * page)
    out = pl.pallas_call(
        kernel,
        grid_spec=pltpu.PrefetchScalarGridSpec(
            num_scalar_prefetch=1,
            grid=(bsz, steps),
            in_specs=[seq(), seq(), seq(), pl.BlockSpec(bias_tile.shape, lambda b, j, pt: (0, 0))]
            + [page_spec(i) for i in range(pp)] * 2,
            out_specs=seq(),
            scratch_shapes=[pltpu.VMEM((d, page), F32), pltpu.VMEM((d, page), F32),
                            pltpu.VMEM((N_HEADS, page), F32), pltpu.VMEM((N_HEADS * pp, page), F32)],
        ),
        out_shape=jax.ShapeDtypeStruct((bsz, 1, d), F32),
        compiler_params=_params("parallel", "arbitrary"),
    )(page_table, q.reshape(bsz, 1, d), k_new.reshape(bsz, 1, d), v_new.reshape(bsz, 1, d), bias_tile,
      *([cache_kt] * pp), *([cache_vt] * pp))
    return out.reshape(bsz, d)


def _outproj_kernel(x_ref, h_ref, gl_ref, o_ref, gs_ref, w_ref, g_ref, y_ref):
    d_lru = h_ref.shape[-1]
    y_lru = (h_ref[...] * jax.nn.silu(gl_ref[...])).astype(BF16)
    y_sb = (o_ref[...] * jax.nn.silu(gs_ref[...])).astype(BF16)
    m = _dot(y_lru, w_ref[pl.ds(0, d_lru), :]) + _dot(y_sb, w_ref[pl.ds(d_lru, w_ref.shape[0] - d_lru), :])
    n = m * lax.rsqrt(jnp.mean(m * m, axis=-1, keepdims=True) + RMS_EPS)
    y_ref[...] = x_ref[...] + n * g_ref[...]


def _outproj(x2d, h, g_lru, o, g_sb, w_bf16, g):
    n, d = x2d.shape
    tm = min(ROW_TILE, n)
    assert n % tm == 0
    rows = lambda w: pl.BlockSpec((tm, w), lambda i: (i, 0))
    return pl.pallas_call(
        _outproj_kernel,
        grid=(n // tm,),
        in_specs=[rows(d), rows(h.shape[1]), rows(g_lru.shape[1]), rows(o.shape[1]), rows(g_sb.shape[1]),
                  pl.BlockSpec(w_bf16.shape, lambda i: (0, 0)),
                  pl.BlockSpec((1, d), lambda i: (0, 0))],
        out_specs=rows(d),
        out_shape=jax.ShapeDtypeStruct((n, d), F32),
        compiler_params=_params("parallel"),
    )(x2d, h, g_lru, o, g_sb, w_bf16, g.reshape(1, d))


def _gate_weights(w_r, w_i):
    nb, blk, _ = w_r.shape
    half_blocks = nb // 2

    def halves(w):
        eye = jnp.eye(half_blocks, dtype=w.dtype)
        w = w.reshape(2, half_blocks, blk, blk)
        return jnp.einsum('cnij,nm->cnimj', w, eye).reshape(2, half_blocks * blk, half_blocks * blk)

    return jnp.concatenate([halves(w_r), halves(w_i)], axis=-1).astype(BF16)


def _feature_major(a):
    lead = a.shape[:-3]
    n = len(lead)
    return jnp.transpose(a, (*range(n), n + 1, n + 2, n)).reshape(*lead, a.shape[-2] * a.shape[-1], a.shape[-3])


def _position_major(a, heads):
    lead = a.shape[:-2]
    n = len(lead)
    a = a.reshape(*lead, heads, a.shape[-2] // heads, a.shape[-1])
    return jnp.transpose(a, (*range(n), n + 2, n, n + 1))


def kernel(x_prompt, x_sample, cache_k, cache_v, state_lru_h, state_conv, page_table, norm_pre_g, norm_post_g, w_in, conv_w, conv_b, w_rgate, b_rgate, w_igate, b_igate, lru_lambda, sb_bias, w_out):
    bsz, seq, d_model = x_prompt.shape
    dec_b, dec_s, _ = x_sample.shape
    depth = w_in.shape[0]
    d_lru = lru_lambda.shape[1]
    assert dec_s == 1 and d_lru == D_HEADS
    width = d_lru
    ckt = _feature_major(cache_k)
    cvt = _feature_major(cache_v)
    col = lambda i: slice(i * width, (i + 1) * width)

    xp = x_prompt
    xs = x_sample.reshape(1, dec_b * dec_s, d_model)
    zero_conv = jnp.zeros((bsz, CONV_W - 1, d_lru), F32)
    zero_h = jnp.zeros((bsz, d_lru), F32)
    kp_l, vp_l, hp_l, cp_l, ks_l, vs_l, hs_l, cs_l = ([] for _ in range(8))
    for l in range(depth):
        w_l = w_in[l].astype(BF16)
        w_ugqg = jnp.concatenate([w_l[:, col(0)], w_l[:, col(1)], w_l[:, col(2)], w_l[:, col(5)]], axis=1)
        w_kv_t = jnp.transpose(w_l[:, 3 * width:5 * width])
        w_out_l = w_out[l].astype(BF16)
        wg = _gate_weights(w_rgate[l], w_igate[l])
        lru_p = (conv_w[l], conv_b[l], wg, b_rgate[l], b_igate[l], lru_lambda[l])

        u, g_lru, q, g_sb, kt, vt = _inproj(xp, norm_pre_g[l], w_ugqg, w_kv_t, width)
        h, h_last, buf = _lru_prompt(u, zero_conv, zero_h, *lru_p)
        o = _sb_prompt(q, kt, vt, sb_bias[l])
        flat = lambda a: a.reshape(bsz * seq, a.shape[-1])
        xp = _outproj(flat(xp), flat(h), flat(g_lru), flat(o), flat(g_sb), w_out_l,
                      norm_post_g[l]).reshape(bsz, seq, d_model)
        kp_l.append(kt)
        vp_l.append(vt)
        hp_l.append(h_last.reshape(bsz, d_lru))
        cp_l.append(buf)

        u, g_lru, q, k, v, g_sb = (a[0] for a in _inproj(xs, norm_pre_g[l], w_l, None, width))
        h, buf_t = _lru_step(u, jnp.swapaxes(state_conv[l], 0, 1), state_lru_h[l], *lru_p)
        o = _sb_paged(q, k, v, ckt, cvt, l, page_table, sb_bias[l])
        xs = _outproj(xs[0], h, g_lru, o, g_sb, w_out_l, norm_post_g[l])[None]
        ks_l.append(k.reshape(dec_b, dec_s, N_HEADS, HEAD_DIM))
        vs_l.append(v.reshape(dec_b, dec_s, N_HEADS, HEAD_DIM))
        hs_l.append(h)
        cs_l.append(buf_t)

    return (xp, xs.reshape(dec_b, dec_s, d_model),
            _position_major(jnp.stack(kp_l), N_HEADS), _position_major(jnp.stack(vp_l), N_HEADS),
            jnp.stack(hp_l), jnp.stack(cp_l),
            jnp.stack(ks_l), jnp.stack(vs_l), jnp.stack(hs_l), jnp.swapaxes(jnp.stack(cs_l), 1, 2))
```

```python
import functools

import jax
import jax.numpy as jnp
from jax import lax
from jax.experimental import pallas as pl
from jax.experimental.pallas import tpu as pltpu

F32 = jnp.float32
BF16 = jnp.bfloat16

RMS_EPS = 1e-6
LRU_C = 8.0
CONV_W = 4
N_HEADS = 8
HEAD_DIM = 64
D_HEADS = N_HEADS * HEAD_DIM

LANES = 128
SUBLANES = 8
VMEM_LIMIT_BYTES = 56 * 1024 * 1024

ROW_TILE = 1024
PROMPT_TILE = 512
ATT_TILE = 256
PAGES_PER_STEP = 32
SPLIT_PARTS = 3


def _params(*semantics):
    return pltpu.CompilerParams(dimension_semantics=semantics, vmem_limit_bytes=VMEM_LIMIT_BYTES)


def _softplus(z):
    return jnp.maximum(z, 0.0) + jnp.log1p(jnp.exp(-jnp.abs(z)))


LOG2_E = 1.4426950408889634


def _log2_one_minus_sigmoid(z2):
    n2 = -z2
    return jnp.minimum(n2, 0.0) - jnp.log2(1.0 + jnp.exp2(jnp.minimum(z2, n2)))


def _dot(a, b):
    return jnp.dot(a, b, preferred_element_type=F32)


def _dot_nt(a, b):
    return lax.dot_general(a, b, (((1,), (1,)), ((), ())), preferred_element_type=F32)


def _split_bf16(x, parts):
    pieces, r = [], x
    for _ in range(parts):
        p = r.astype(BF16)
        pieces.append(p)
        r = r - p.astype(F32)
    return pieces


def _dot_f32_by_01(x, w01, parts=SPLIT_PARTS):
    return sum(_dot(p, w01) for p in _split_bf16(x, parts))


def _dot_01_by_f32(w01, x, parts=SPLIT_PARTS):
    return sum(_dot(w01, p) for p in _split_bf16(x, parts))


def _sigmoid(x):
    return 0.5 * (jnp.tanh(0.5 * x) + 1.0)


def _rms_norm_bf16(x_ref, g_ref):
    x = x_ref[...]
    y = x * lax.rsqrt(jnp.mean(x * x, axis=-1, keepdims=True) + RMS_EPS)
    return (y * g_ref[...]).astype(BF16)


def _inproj_kernel(x_ref, g_ref, w_ref, *out_refs):
    xn = _rms_norm_bf16(x_ref, g_ref)
    for i, o_ref in enumerate(out_refs):
        width = o_ref.shape[-1]
        o_ref[...] = _dot(xn, w_ref[:, i * width:(i + 1) * width])


def _inproj(x2d, g, w_bf16, width):
    n, d = x2d.shape
    n_out = w_bf16.shape[1] // width
    return pl.pallas_call(
        _inproj_kernel,
        out_shape=[jax.ShapeDtypeStruct((n, width), F32)] * n_out,
        compiler_params=pltpu.CompilerParams(vmem_limit_bytes=VMEM_LIMIT_BYTES),
    )(x2d, g.reshape(1, d), w_bf16)


def _lru_gate_logits(xc, wg_ref):
    half = xc.shape[-1] // 2
    xb = xc.astype(BF16)
    g0 = _dot(xb[:, :half], wg_ref[0])
    g1 = _dot(xb[:, half:], wg_ref[1])
    return (jnp.concatenate([g0[:, :half], g1[:, :half]], axis=-1),
            jnp.concatenate([g0[:, half:], g1[:, half:]], axis=-1))


def _lru_coefficients(xc, r_logit, i_logit, br, bi, lam):
    r = _sigmoid(r_logit + br)
    i = _sigmoid(i_logit + bi)
    log_a = LRU_C * r * (-_softplus(-lam))
    a = jnp.exp(log_a)
    b = jnp.sqrt(-jnp.tanh(log_a) * (a * a + 1.0)) * (i * xc)
    return a, b


def _inproj_lru_kernel(x_ref, g_ref, w_ref, wt_ref, conv0_ref, h0_ref, cw_ref, cb_ref, wg_ref, br_ref, bi_ref,
                       lam_ref, *refs):
    gl_ref, q_ref, gs_ref, kt_ref, vt_ref, h_ref, hlast_ref, buf_ref, ext_ref, a_ref, b_ref, hc_ref = refs
    tc, d = h_ref.shape
    pad = SUBLANES
    hist = CONV_W - 1

    @pl.when(pl.program_id(1) == 0)
    def _():
        hc_ref[...] = h0_ref[...]
        ext_ref[pl.ds(pad - hist, hist), :] = conv0_ref[...]

    xn = _rms_norm_bf16(x_ref, g_ref)
    ext_ref[pl.ds(pad, tc), :] = _dot(xn, w_ref[:, 0:d])
    for i, o_ref in enumerate((gl_ref, q_ref, gs_ref)):
        o_ref[...] = _dot(xn, w_ref[:, (i + 1) * d:(i + 2) * d])
    for i, o_ref in enumerate((kt_ref, vt_ref)):
        o_ref[...] = _dot_nt(wt_ref[pl.ds(i * d, d), :], xn)

    xc = cb_ref[...] + ext_ref[pl.ds(pad - hist, tc), :] * cw_ref[pl.ds(0, 1), :]
    for j in range(1, CONV_W):
        xc = xc + ext_ref[pl.ds(pad - hist + j, tc), :] * cw_ref[pl.ds(j, 1), :]
    tail = ext_ref[pl.ds(pad + tc - hist, hist), :]
    ext_ref[pl.ds(pad - hist, hist), :] = tail
    buf_ref[...] = tail
    a, b = _lru_coefficients(xc, *_lru_gate_logits(xc, wg_ref), br_ref[...], bi_ref[...], lam_ref[...])

    groups = tc // SUBLANES
    ga = a.reshape(groups, SUBLANES, d)
    gb = b.reshape(groups, SUBLANES, d)
    step_in_group = lax.broadcasted_iota(jnp.int32, ga.shape, 1)
    for k in (1 << e for e in range(SUBLANES.bit_length() - 1)):
        has_prev = step_in_group >= k
        gb = jnp.where(has_prev, ga * pltpu.roll(gb, k, axis=1) + gb, gb)
        ga = jnp.where(has_prev, ga * pltpu.roll(ga, k, axis=1), ga)
    a_ref[...] = ga.reshape(tc, d)
    b_ref[...] = gb.reshape(tc, d)

    def group(i, h):
        rows = pl.ds(pl.multiple_of(i * SUBLANES, SUBLANES), SUBLANES)
        hg = a_ref[rows, :] * h + b_ref[rows, :]
        h_ref[rows, :] = hg
        return hg[SUBLANES - 1:SUBLANES, :]

    h = lax.fori_loop(0, groups, group, hc_ref[...], unroll=4)
    hc_ref[...] = h
    hlast_ref[...] = h


def _inproj_lru(x, g, w_ugqg, w_kv_t, conv0, h0, cw, cb, wg, br, bi, lam):
    bsz, t, d = x.shape
    width = lam.shape[-1]
    tm = min(PROMPT_TILE, t)
    hist = CONV_W - 1
    assert t % tm == 0 and tm >= hist and tm % SUBLANES == 0
    full = lambda a: pl.BlockSpec(a.shape, lambda b, i: (0,) * a.ndim)
    vec = lambda: pl.BlockSpec((1, width), lambda b, i: (0, 0))
    rows = lambda: pl.BlockSpec((None, tm, width), lambda b, i: (b, i, 0))
    per_seq = lambda n: pl.BlockSpec((None, n, width), lambda b, i: (b, 0, 0))
    cols = lambda: pl.BlockSpec((None, width, tm), lambda b, i: (b, 0, i))
    operands = [x, g.reshape(1, d), w_ugqg, w_kv_t, conv0, h0.reshape(bsz, 1, width), cw, cb.reshape(1, width), wg,
                br.reshape(1, width), bi.reshape(1, width), lam.reshape(1, width)]
    in_specs = [pl.BlockSpec((None, tm, d), lambda b, i: (b, i, 0)), pl.BlockSpec((1, d), lambda b, i: (0, 0)),
                full(w_ugqg), full(w_kv_t), per_seq(hist), per_seq(1), full(cw), vec(), full(wg), vec(), vec(), vec()]
    seq_state = lambda n: jax.ShapeDtypeStruct((bsz, n, width), F32)
    return pl.pallas_call(
        _inproj_lru_kernel,
        grid=(bsz, t // tm),
        in_specs=in_specs,
        out_specs=[rows(), rows(), rows(), cols(), cols(), rows(), per_seq(1), per_seq(hist)],
        out_shape=[seq_state(t)] * 3 + [jax.ShapeDtypeStruct((bsz, width, t), F32)] * 2
        + [seq_state(t), seq_state(1), seq_state(hist)],
        scratch_shapes=[
            pltpu.VMEM((tm + SUBLANES, width), F32),
            pltpu.VMEM((tm, width), F32),
            pltpu.VMEM((tm, width), F32),
            pltpu.VMEM((1, width), F32),
        ],
        compiler_params=_params("parallel", "arbitrary"),
    )(*operands)


def _lru_step_kernel(u_ref, conv_ref, h0_ref, cw_ref, cb_ref, wg_ref, br_ref, bi_ref, lam_ref,
                     h_ref, buf_ref):
    u = u_ref[...]
    xc = cb_ref[...] + u * cw_ref[pl.ds(CONV_W - 1, 1), :]
    for j in range(CONV_W - 1):
        xc = xc + conv_ref[j] * cw_ref[pl.ds(j, 1), :]
    a, b = _lru_coefficients(xc, *_lru_gate_logits(xc, wg_ref), br_ref[...], bi_ref[...], lam_ref[...])
    h_ref[...] = a * h0_ref[...] + b
    for j in range(CONV_W - 2):
        buf_ref[j] = conv_ref[j + 1]
    buf_ref[CONV_W - 2] = u


def _lru_step(u, conv_t, h0, cw, cb, wg, br, bi, lam):
    bsz, d = u.shape
    return pl.pallas_call(
        _lru_step_kernel,
        out_shape=[jax.ShapeDtypeStruct((bsz, d), F32), jax.ShapeDtypeStruct(conv_t.shape, F32)],
        compiler_params=pltpu.CompilerParams(vmem_limit_bytes=VMEM_LIMIT_BYTES),
    )(u, conv_t, h0, cw, cb.reshape(1, d), wg, br.reshape(1, d), bi.reshape(1, d), lam.reshape(1, d))


def _sb_prompt_kernel(bias_ref, tri_ref, q_ref, kt_ref, vt_ref, o_ref, qm_ref, pre_ref, total_ref, carry_ref):
    tq = q_ref.shape[0]
    tk = tri_ref.shape[0]
    qi = pl.program_id(1)
    pair = 2 * HEAD_DIM
    scale = HEAD_DIM ** -0.5

    lane_head = lax.broadcasted_iota(jnp.int32, (tq, pair), 1) // HEAD_DIM
    for h in range(N_HEADS):
        qp = q_ref[:, pl.ds((h // 2) * pair, pair)] * scale
        qm_ref[h] = jnp.where(lane_head == h % 2, qp, 0.0).astype(BF16)
    row_head = lax.broadcasted_iota(jnp.int32, (pair, tk), 0) // HEAD_DIM
    visible = (lax.broadcasted_iota(jnp.int32, (tq, tk), 1) < lax.broadcasted_iota(jnp.int32, (tq, tk), 0))

    def keys_of(j):
        return pl.ds(pl.multiple_of(j * tk, tk), tk)

    def front(j, diagonal):
        for p in range(N_HEADS // 2):
            ktp = kt_ref[pl.ds(p * pair, pair), keys_of(j)].astype(BF16)
            for s in range(2):
                h = 2 * p + s
                z = (_dot(qm_ref[h], ktp) + bias_ref[h]) * LOG2_E
                m = _log2_one_minus_sigmoid(z)
                if diagonal:
                    m = jnp.where(visible, m, 0.0)
                csum = _dot(m.astype(BF16), tri_ref[...])
                pre = z + csum
                pre_ref[h] = jnp.where(visible, pre, -jnp.inf) if diagonal else pre
                total_ref[h] = csum[:, 0:1]

    def back(j):
        for p in range(N_HEADS // 2):
            feats = pl.ds(p * pair, pair)
            vtp = vt_ref[feats, keys_of(j)].astype(BF16)
            out = None
            for s in range(2):
                h = 2 * p + s
                carry = carry_ref[h]
                w = jnp.exp2(pre_ref[h] + carry)
                pv = _dot_nt(w.astype(BF16), jnp.where(row_head == s, vtp, jnp.zeros_like(vtp)))
                out = pv if out is None else out + pv
                carry_ref[h] = carry + total_ref[h]
            o_ref[:, feats] += out

    o_ref[...] = jnp.zeros(o_ref.shape, F32)
    carry_ref[...] = jnp.zeros(carry_ref.shape, F32)
    front(qi, True)

    def trip(i, c):
        back(qi - i + 1)
        front(qi - i, False)
        return c

    lax.fori_loop(1, qi + 1, trip, 0)
    back(0)


def _sb_prompt(q, kt, vt, bias):
    bsz, t, d = q.shape
    tq = min(ATT_TILE, t)
    assert t % tq == 0 and d == D_HEADS
    tri = (jnp.arange(tq)[:, None] >= jnp.arange(tq)[None, :]).astype(BF16)
    return pl.pallas_call(
        _sb_prompt_kernel,
        grid=(bsz, t // tq),
        in_specs=[
            pl.BlockSpec(memory_space=pltpu.SMEM),
            pl.BlockSpec((tq, tq), lambda b, i: (0, 0)),
            pl.BlockSpec((None, tq, d), lambda b, i: (b, i, 0)),
            pl.BlockSpec((None, d, t), lambda b, i: (b, 0, 0)),
            pl.BlockSpec((None, d, t), lambda b, i: (b, 0, 0)),
        ],
        out_specs=pl.BlockSpec((None, tq, d), lambda b, i: (b, i, 0)),
        out_shape=jax.ShapeDtypeStruct((bsz, t, d), F32),
        scratch_shapes=[pltpu.VMEM((N_HEADS, tq, 2 * HEAD_DIM), BF16), pltpu.VMEM((N_HEADS, tq, tq), F32),
                        pltpu.VMEM((N_HEADS, tq, 1), F32), pltpu.VMEM((N_HEADS, tq, 1), F32)],
        compiler_params=_params("parallel", "arbitrary"),
    )(bias, tri, q, kt, vt)


def _sb_paged_kernel(pt_ref, q_ref, knew_ref, vnew_ref, bias_ref, *refs, pages_per_step, past_len):
    del pt_ref
    pp = pages_per_step
    k_refs, v_refs = refs[:pp], refs[pp:2 * pp]
    o_ref, qcol_ref, acc_ref, carry_ref, w_ref = refs[2 * pp:]
    j = pl.program_id(1)
    d, page = k_refs[0].shape
    rows = pp * N_HEADS
    scale = HEAD_DIM ** -0.5

    @pl.when(j == 0)
    def _():
        q = q_ref[...]
        qcol_ref[...] = jnp.transpose(jnp.broadcast_to(q, (page, d)))
        assert knew_ref.shape[0] == 1
        new_idx = jnp.zeros((N_HEADS, page), jnp.int32)
        mask = (past_len + new_idx) < (past_len + new_idx)
        own = (lax.broadcasted_iota(jnp.int32, (N_HEADS, d), 0)
               == lax.broadcasted_iota(jnp.int32, (N_HEADS, d), 1) // HEAD_DIM)
        z = jnp.sum(jnp.where(own, q * knew_ref[...], 0.0), axis=-1, keepdims=True) * scale
        head_bias = jnp.concatenate([bias_ref[pl.ds(h * pp, 1), :] for h in range(N_HEADS)], axis=0)
        z = (z + head_bias) * LOG2_E
        carry_ref[...] = jnp.where(mask, _log2_one_minus_sigmoid(z), 0.0)
        w = jnp.where(mask, jnp.exp2(z), 0.0)
        vnew_col = jnp.transpose(jnp.broadcast_to(vnew_ref[...], (page, d)))
        acc_ref[...] = (vnew_col.reshape(N_HEADS, HEAD_DIM, page) * w[:, None, :]).reshape(d, page)

    z_rows = []
    for h in range(N_HEADS):
        feats = pl.ds(h * HEAD_DIM, HEAD_DIM)
        qh = qcol_ref[feats, :]
        z_rows += [jnp.sum(k_refs[i][feats, :] * qh, axis=0, keepdims=True) for i in range(pp)]
    z = jnp.concatenate(z_rows, axis=0)
    z = (z * scale + bias_ref[...]) * LOG2_E
    m = _log2_one_minus_sigmoid(z)
    ki = lax.broadcasted_iota(jnp.int32, (page, page), 0)
    si = lax.broadcasted_iota(jnp.int32, (page, page), 1)
    tri = (ki >= si).astype(BF16)
    ri = lax.broadcasted_iota(jnp.int32, (rows, rows), 0)
    ci = lax.broadcasted_iota(jnp.int32, (rows, rows), 1)
    later = ((ri // pp == ci // pp) & (ci > ri)).astype(BF16)
    csum = _dot_f32_by_01(m, tri)
    total = jnp.broadcast_to(csum[:, 0:1], csum.shape)
    after = _dot_01_by_f32(later, total)
    carry = carry_ref[...]
    carry_rows = jnp.concatenate([jnp.broadcast_to(carry[h:h + 1, :], (pp, page)) for h in range(N_HEADS)], axis=0)
    w_ref[...] = jnp.exp2(z + csum + after + carry_rows)
    for h in range(N_HEADS):
        feats = pl.ds(h * HEAD_DIM, HEAD_DIM)
        acc = acc_ref[feats, :]
        for i in range(pp):
            acc = acc + v_refs[i][feats, :] * w_ref[pl.ds(h * pp + i, 1), :]
        acc_ref[feats, :] = acc
    first = after + total
    carry_ref[...] = carry + jnp.concatenate([first[h * pp:h * pp + 1, :] for h in range(N_HEADS)], axis=0)

    @pl.when(j == pl.num_programs(1) - 1)
    def _():
        o_ref[...] = jnp.sum(jnp.transpose(acc_ref[...]), axis=0, keepdims=True)


def _sb_paged(q, k_new, v_new, cache_kt, cache_vt, layer, page_table, bias):
    bsz, d = q.shape
    n_pages = page_table.shape[1]
    page = cache_kt.shape[3]
    assert d == D_HEADS and page == LANES
    pp = min(PAGES_PER_STEP, n_pages)
    assert n_pages % pp == 0
    steps = n_pages // pp
    bias_tile = jnp.broadcast_to(bias.reshape(N_HEADS, 1, 1), (N_HEADS, pp, page)).reshape(N_HEADS * pp, page)

    def page_spec(i):
        return pl.BlockSpec((None, None, d, page),
                            lambda b, j, pt: (layer, pt[b, n_pages - pp * (j + 1) + i], 0, 0))

    seq = lambda: pl.BlockSpec((None, 1, d), lambda b, j, pt: (b, 0, 0))
    kernel = functools.partial(_sb_paged_kernel, pages_per_step=pp, past_len=n_pages * page)
    out = pl.pallas_call(
        kernel,
        grid_spec=pltpu.PrefetchScalarGridSpec(
            num_scalar_prefetch=1,
            grid=(bsz, steps),
            in_specs=[seq(), seq(), seq(), pl.BlockSpec(bias_tile.shape, lambda b, j, pt: (0, 0))]
            + [page_spec(i) for i in range(pp)] * 2,
            out_specs=seq(),
            scratch_shapes=[pltpu.VMEM((d, page), F32), pltpu.VMEM((d, page), F32),
                            pltpu.VMEM((N_HEADS, page), F32), pltpu.VMEM((N_HEADS * pp, page), F32)],
        ),
        out_shape=jax.ShapeDtypeStruct((bsz, 1, d), F32),
        compiler_params=_params("parallel", "arbitrary"),
    )(page_table, q.reshape(bsz, 1, d), k_new.reshape(bsz, 1, d), v_new.reshape(bsz, 1, d), bias_tile,
      *([cache_kt] * pp), *([cache_vt] * pp))
    return out.reshape(bsz, d)


def _outproj_kernel(x_ref, h_ref, gl_ref, o_ref, gs_ref, w_ref, g_ref, y_ref):
    d_lru = h_ref.shape[-1]
    y_lru = (h_ref[...] * jax.nn.silu(gl_ref[...])).astype(BF16)
    y_sb = (o_ref[...] * jax.nn.silu(gs_ref[...])).astype(BF16)
    m = _dot(y_lru, w_ref[pl.ds(0, d_lru), :]) + _dot(y_sb, w_ref[pl.ds(d_lru, w_ref.shape[0] - d_lru), :])
    n = m * lax.rsqrt(jnp.mean(m * m, axis=-1, keepdims=True) + RMS_EPS)
    y_ref[...] = x_ref[...] + n * g_ref[...]


def _outproj(x2d, h, g_lru, o, g_sb, w_bf16, g):
    n, d = x2d.shape
    tm = min(ROW_TILE, n)
    assert n % tm == 0
    rows = lambda w: pl.BlockSpec((tm, w), lambda i: (i, 0))
    return pl.pallas_call(
        _outproj_kernel,
        grid=(n // tm,),
        in_specs=[rows(d), rows(h.shape[1]), rows(g_lru.shape[1]), rows(o.shape[1]), rows(g_sb.shape[1]),
                  pl.BlockSpec(w_bf16.shape, lambda i: (0, 0)),
                  pl.BlockSpec((1, d), lambda i: (0, 0))],
        out_specs=rows(d),
        out_shape=jax.ShapeDtypeStruct((n, d), F32),
        compiler_params=_params("parallel"),
    )(x2d, h, g_lru, o, g_sb, w_bf16, g.reshape(1, d))


def _gate_weights(w_r, w_i):
    nb, blk, _ = w_r.shape
    half_blocks = nb // 2

    def halves(w):
        eye = jnp.eye(half_blocks, dtype=w.dtype)
        w = w.reshape(2, half_blocks, blk, blk)
        return jnp.einsum('cnij,nm->cnimj', w, eye).reshape(2, half_blocks * blk, half_blocks * blk)

    return jnp.concatenate([halves(w_r), halves(w_i)], axis=-1).astype(BF16)


def _feature_major(a):
    lead = a.shape[:-3]
    n = len(lead)
    return jnp.transpose(a, (*range(n), n + 1, n + 2, n)).reshape(*lead, a.shape[-2] * a.shape[-1], a.shape[-3])


def _position_major(a, heads):
    lead = a.shape[:-2]
    n = len(lead)
    a = a.reshape(*lead, heads, a.shape[-2] // heads, a.shape[-1])
    return jnp.transpose(a, (*range(n), n + 2, n, n + 1))


def kernel(x_prompt, x_sample, cache_k, cache_v, state_lru_h, state_conv, page_table, norm_pre_g, norm_post_g, w_in, conv_w, conv_b, w_rgate, b_rgate, w_igate, b_igate, lru_lambda, sb_bias, w_out):
    bsz, seq, d_model = x_prompt.shape
    dec_b, dec_s, _ = x_sample.shape
    depth = w_in.shape[0]
    d_lru = lru_lambda.shape[1]
    assert dec_s == 1 and d_lru == D_HEADS
    width = d_lru
    ckt = _feature_major(cache_k)
    cvt = _feature_major(cache_v)
    col = lambda i: slice(i * width, (i + 1) * width)

    xp = x_prompt
    xs = x_sample.reshape(dec_b * dec_s, d_model)
    zero_conv = jnp.zeros((bsz, CONV_W - 1, d_lru), F32)
    zero_h = jnp.zeros((bsz, d_lru), F32)
    kp_l, vp_l, hp_l, cp_l, ks_l, vs_l, hs_l, cs_l = ([] for _ in range(8))
    for l in range(depth):
        w_l = w_in[l].astype(BF16)
        w_ugqg = jnp.concatenate([w_l[:, col(0)], w_l[:, col(1)], w_l[:, col(2)], w_l[:, col(5)]], axis=1)
        w_kv_t = jnp.transpose(w_l[:, 3 * width:5 * width])
        w_out_l = w_out[l].astype(BF16)
        wg = _gate_weights(w_rgate[l], w_igate[l])
        lru_p = (conv_w[l], conv_b[l], wg, b_rgate[l], b_igate[l], lru_lambda[l])

        g_lru, q, g_sb, kt, vt, h, h_last, buf = _inproj_lru(
            xp, norm_pre_g[l], w_ugqg, w_kv_t, zero_conv, zero_h, *lru_p)
        o = _sb_prompt(q, kt, vt, sb_bias[l])
        flat = lambda a: a.reshape(bsz * seq, a.shape[-1])
        xp = _outproj(flat(xp), flat(h), flat(g_lru), flat(o), flat(g_sb), w_out_l,
                      norm_post_g[l]).reshape(bsz, seq, d_model)
        kp_l.append(kt)
        vp_l.append(vt)
        hp_l.append(h_last.reshape(bsz, d_lru))
        cp_l.append(buf)

        u, g_lru, q, k, v, g_sb = _inproj(xs, norm_pre_g[l], w_l, width)
        h, buf_t = _lru_step(u, jnp.swapaxes(state_conv[l], 0, 1), state_lru_h[l], *lru_p)
        o = _sb_paged(q, k, v, ckt, cvt, l, page_table, sb_bias[l])
        xs = _outproj(xs, h, g_lru, o, g_sb, w_out_l, norm_post_g[l])
        ks_l.append(k.reshape(dec_b, dec_s, N_HEADS, HEAD_DIM))
        vs_l.append(v.reshape(dec_b, dec_s, N_HEADS, HEAD_DIM))
        hs_l.append(h)
        cs_l.append(buf_t)

    return (xp, xs.reshape(dec_b, dec_s, d_model),
            _position_major(jnp.stack(kp_l), N_HEADS), _position_major(jnp.stack(vp_l), N_HEADS),
            jnp.stack(hp_l), jnp.stack(cp_l),
            jnp.stack(ks_l), jnp.stack(vs_l), jnp.stack(hs_l), jnp.swapaxes(jnp.stack(cs_l), 1, 2))
```

```python
import functools

import jax
import jax.numpy as jnp
from jax import lax
from jax.experimental import pallas as pl
from jax.experimental.pallas import tpu as pltpu

F32 = jnp.float32
BF16 = jnp.bfloat16

RMS_EPS = 1e-6
LRU_C = 8.0
CONV_W = 4
N_HEADS = 8
HEAD_DIM = 64
D_HEADS = N_HEADS * HEAD_DIM

LANES = 128
SUBLANES = 8
VMEM_LIMIT_BYTES = 56 * 1024 * 1024

ROW_TILE = 1024
PROMPT_TILE = 512
ATT_TILE = 256
PAGES_PER_STEP = 32
SPLIT_PARTS = 3


def _params(*semantics):
    return pltpu.CompilerParams(dimension_semantics=semantics, vmem_limit_bytes=VMEM_LIMIT_BYTES)


def _softplus(z):
    return jnp.maximum(z, 0.0) + jnp.log1p(jnp.exp(-jnp.abs(z)))


LOG2_E = 1.4426950408889634


def _log2_one_minus_sigmoid(z2):
    n2 = -z2
    return jnp.minimum(n2, 0.0) - jnp.log2(1.0 + jnp.exp2(jnp.minimum(z2, n2)))


def _dot(a, b):
    return jnp.dot(a, b, preferred_element_type=F32)


def _dot_nt(a, b):
    return lax.dot_general(a, b, (((1,), (1,)), ((), ())), preferred_element_type=F32)


def _split_bf16(x, parts):
    pieces, r = [], x
    for _ in range(parts):
        p = r.astype(BF16)
        pieces.append(p)
        r = r - p.astype(F32)
    return pieces


def _dot_f32_by_01(x, w01, parts=SPLIT_PARTS):
    return sum(_dot(p, w01) for p in _split_bf16(x, parts))


def _dot_01_by_f32(w01, x, parts=SPLIT_PARTS):
    return sum(_dot(w01, p) for p in _split_bf16(x, parts))


def _sigmoid(x):
    return 0.5 * (jnp.tanh(0.5 * x) + 1.0)


def _rms_norm_bf16(x, g):
    y = x * lax.rsqrt(jnp.mean(x * x, axis=-1, keepdims=True) + RMS_EPS)
    return (y * g).astype(BF16)


def _inproj_kernel(x_ref, g_ref, w_ref, *out_refs):
    xn = _rms_norm_bf16(x_ref[...], g_ref[...])
    for i, o_ref in enumerate(out_refs):
        width = o_ref.shape[-1]
        o_ref[...] = _dot(xn, w_ref[:, i * width:(i + 1) * width])


def _inproj(x2d, g, w_bf16, width):
    n, d = x2d.shape
    n_out = w_bf16.shape[1] // width
    return pl.pallas_call(
        _inproj_kernel,
        out_shape=[jax.ShapeDtypeStruct((n, width), F32)] * n_out,
        compiler_params=pltpu.CompilerParams(vmem_limit_bytes=VMEM_LIMIT_BYTES),
    )(x2d, g.reshape(1, d), w_bf16)


def _lru_gate_logits(xc, wg_ref):
    half = xc.shape[-1] // 2
    xb = xc.astype(BF16)
    g0 = _dot(xb[:, :half], wg_ref[0])
    g1 = _dot(xb[:, half:], wg_ref[1])
    return (jnp.concatenate([g0[:, :half], g1[:, :half]], axis=-1),
            jnp.concatenate([g0[:, half:], g1[:, half:]], axis=-1))


def _lru_coefficients(xc, r_logit, i_logit, br, bi, lam):
    r = _sigmoid(r_logit + br)
    i = _sigmoid(i_logit + bi)
    log_a = LRU_C * r * (-_softplus(-lam))
    a = jnp.exp(log_a)
    b = jnp.sqrt(-jnp.tanh(log_a) * (a * a + 1.0)) * (i * xc)
    return a, b


def _inproj_lru_kernel(*refs, after_outproj):
    if after_outproj:
        x = _outproj_rows(*refs[:7])
        refs = refs[7:]
    else:
        x = refs[0][...]
        refs = refs[1:]
    g_ref, w_ref, wt_ref, conv0_ref, h0_ref, cw_ref, cb_ref, wg_ref, br_ref, bi_ref, lam_ref = refs[:11]
    refs = refs[11:]
    if after_outproj:
        refs[0][...] = x
        refs = refs[1:]
    gl_ref, q_ref, gs_ref, kt_ref, vt_ref, h_ref, hlast_ref, buf_ref, ext_ref, a_ref, b_ref, hc_ref = refs
    tc, d = h_ref.shape
    pad = SUBLANES
    hist = CONV_W - 1

    @pl.when(pl.program_id(1) == 0)
    def _():
        hc_ref[...] = h0_ref[...]
        ext_ref[pl.ds(pad - hist, hist), :] = conv0_ref[...]

    xn = _rms_norm_bf16(x, g_ref[...])
    ext_ref[pl.ds(pad, tc), :] = _dot(xn, w_ref[:, 0:d])
    for i, o_ref in enumerate((gl_ref, q_ref, gs_ref)):
        o_ref[...] = _dot(xn, w_ref[:, (i + 1) * d:(i + 2) * d])
    for i, o_ref in enumerate((kt_ref, vt_ref)):
        o_ref[...] = _dot_nt(wt_ref[pl.ds(i * d, d), :], xn)

    xc = cb_ref[...] + ext_ref[pl.ds(pad - hist, tc), :] * cw_ref[pl.ds(0, 1), :]
    for j in range(1, CONV_W):
        xc = xc + ext_ref[pl.ds(pad - hist + j, tc), :] * cw_ref[pl.ds(j, 1), :]
    tail = ext_ref[pl.ds(pad + tc - hist, hist), :]
    ext_ref[pl.ds(pad - hist, hist), :] = tail
    buf_ref[...] = tail
    a, b = _lru_coefficients(xc, *_lru_gate_logits(xc, wg_ref), br_ref[...], bi_ref[...], lam_ref[...])

    groups = tc // SUBLANES
    ga = a.reshape(groups, SUBLANES, d)
    gb = b.reshape(groups, SUBLANES, d)
    step_in_group = lax.broadcasted_iota(jnp.int32, ga.shape, 1)
    for k in (1 << e for e in range(SUBLANES.bit_length() - 1)):
        has_prev = step_in_group >= k
        gb = jnp.where(has_prev, ga * pltpu.roll(gb, k, axis=1) + gb, gb)
        ga = jnp.where(has_prev, ga * pltpu.roll(ga, k, axis=1), ga)
    a_ref[...] = ga.reshape(tc, d)
    b_ref[...] = gb.reshape(tc, d)

    def group(i, h):
        rows = pl.ds(pl.multiple_of(i * SUBLANES, SUBLANES), SUBLANES)
        hg = a_ref[rows, :] * h + b_ref[rows, :]
        h_ref[rows, :] = hg
        return hg[SUBLANES - 1:SUBLANES, :]

    h = lax.fori_loop(0, groups, group, hc_ref[...], unroll=4)
    hc_ref[...] = h
    hlast_ref[...] = h


def _inproj_lru(x, g, w_ugqg, w_kv_t, conv0, h0, cw, cb, wg, br, bi, lam, prev=None):
    bsz, t, d = x.shape
    width = lam.shape[-1]
    tm = min(PROMPT_TILE, t)
    hist = CONV_W - 1
    assert t % tm == 0 and tm >= hist and tm % SUBLANES == 0
    full = lambda a: pl.BlockSpec(a.shape, lambda b, i: (0,) * a.ndim)
    vec = lambda n: pl.BlockSpec((1, n), lambda b, i: (0, 0))
    rows = lambda n: pl.BlockSpec((None, tm, n), lambda b, i: (b, i, 0))
    cols = lambda: pl.BlockSpec((None, width, tm), lambda b, i: (b, 0, i))
    per_seq = lambda n: pl.BlockSpec((None, n, width), lambda b, i: (b, 0, 0))
    seq_state = lambda n: jax.ShapeDtypeStruct((bsz, n, width), F32)

    operands, in_specs = [x], [rows(d)]
    out_specs, out_shape = [], []
    if prev is not None:
        h_p, gl_p, o_p, gs_p, w_out, g_post = prev
        operands += [h_p, gl_p, o_p, gs_p, w_out, g_post.reshape(1, d)]
        in_specs += [rows(width)] * 4 + [full(w_out), vec(d)]
        out_specs, out_shape = [rows(d)], [jax.ShapeDtypeStruct((bsz, t, d), F32)]
    operands += [g.reshape(1, d), w_ugqg, w_kv_t, conv0, h0.reshape(bsz, 1, width), cw, cb.reshape(1, width), wg,
                 br.reshape(1, width), bi.reshape(1, width), lam.reshape(1, width)]
    in_specs += [vec(d), full(w_ugqg), full(w_kv_t), per_seq(hist), per_seq(1), full(cw), vec(width), full(wg),
                 vec(width), vec(width), vec(width)]
    out_specs += [rows(width)] * 3 + [cols(), cols(), rows(width), per_seq(1), per_seq(hist)]
    out_shape += [seq_state(t)] * 3 + [jax.ShapeDtypeStruct((bsz, width, t), F32)] * 2 + [
        seq_state(t), seq_state(1), seq_state(hist)]
    return pl.pallas_call(
        functools.partial(_inproj_lru_kernel, after_outproj=prev is not None),
        grid=(bsz, t // tm),
        in_specs=in_specs,
        out_specs=out_specs,
        out_shape=out_shape,
        scratch_shapes=[
            pltpu.VMEM((tm + SUBLANES, width), F32),
            pltpu.VMEM((tm, width), F32),
            pltpu.VMEM((tm, width), F32),
            pltpu.VMEM((1, width), F32),
        ],
        compiler_params=_params("parallel", "arbitrary"),
    )(*operands)


def _lru_step_kernel(u_ref, conv_ref, h0_ref, cw_ref, cb_ref, wg_ref, br_ref, bi_ref, lam_ref,
                     h_ref, buf_ref):
    u = u_ref[...]
    xc = cb_ref[...] + u * cw_ref[pl.ds(CONV_W - 1, 1), :]
    for j in range(CONV_W - 1):
        xc = xc + conv_ref[j] * cw_ref[pl.ds(j, 1), :]
    a, b = _lru_coefficients(xc, *_lru_gate_logits(xc, wg_ref), br_ref[...], bi_ref[...], lam_ref[...])
    h_ref[...] = a * h0_ref[...] + b
    for j in range(CONV_W - 2):
        buf_ref[j] = conv_ref[j + 1]
    buf_ref[CONV_W - 2] = u


def _lru_step(u, conv_t, h0, cw, cb, wg, br, bi, lam):
    bsz, d = u.shape
    return pl.pallas_call(
        _lru_step_kernel,
        out_shape=[jax.ShapeDtypeStruct((bsz, d), F32), jax.ShapeDtypeStruct(conv_t.shape, F32)],
        compiler_params=pltpu.CompilerParams(vmem_limit_bytes=VMEM_LIMIT_BYTES),
    )(u, conv_t, h0, cw, cb.reshape(1, d), wg, br.reshape(1, d), bi.reshape(1, d), lam.reshape(1, d))


def _sb_prompt_kernel(bias_ref, tri_ref, q_ref, kt_ref, vt_ref, o_ref, qm_ref, pre_ref, total_ref, carry_ref):
    tq = q_ref.shape[0]
    tk = tri_ref.shape[0]
    qi = pl.program_id(1)
    pair = 2 * HEAD_DIM
    scale = HEAD_DIM ** -0.5

    lane_head = lax.broadcasted_iota(jnp.int32, (tq, pair), 1) // HEAD_DIM
    for h in range(N_HEADS):
        qp = q_ref[:, pl.ds((h // 2) * pair, pair)] * scale
        qm_ref[h] = jnp.where(lane_head == h % 2, qp, 0.0).astype(BF16)
    row_head = lax.broadcasted_iota(jnp.int32, (pair, tk), 0) // HEAD_DIM
    visible = (lax.broadcasted_iota(jnp.int32, (tq, tk), 1) < lax.broadcasted_iota(jnp.int32, (tq, tk), 0))

    def keys_of(j):
        return pl.ds(pl.multiple_of(j * tk, tk), tk)

    def front(j, diagonal):
        for p in range(N_HEADS // 2):
            ktp = kt_ref[pl.ds(p * pair, pair), keys_of(j)].astype(BF16)
            for s in range(2):
                h = 2 * p + s
                z = (_dot(qm_ref[h], ktp) + bias_ref[h]) * LOG2_E
                m = _log2_one_minus_sigmoid(z)
                if diagonal:
                    m = jnp.where(visible, m, 0.0)
                csum = _dot(m.astype(BF16), tri_ref[...])
                pre = z + csum
                pre_ref[h] = jnp.where(visible, pre, -jnp.inf) if diagonal else pre
                total_ref[h] = csum[:, 0:1]

    def back(j):
        for p in range(N_HEADS // 2):
            feats = pl.ds(p * pair, pair)
            vtp = vt_ref[feats, keys_of(j)].astype(BF16)
            out = None
            for s in range(2):
                h = 2 * p + s
                carry = carry_ref[h]
                w = jnp.exp2(pre_ref[h] + carry)
                pv = _dot_nt(w.astype(BF16), jnp.where(row_head == s, vtp, jnp.zeros_like(vtp)))
                out = pv if out is None else out + pv
                carry_ref[h] = carry + total_ref[h]
            o_ref[:, feats] += out

    o_ref[...] = jnp.zeros(o_ref.shape, F32)
    carry_ref[...] = jnp.zeros(carry_ref.shape, F32)
    front(qi, True)

    def trip(i, c):
        back(qi - i + 1)
        front(qi - i, False)
        return c

    lax.fori_loop(1, qi + 1, trip, 0)
    back(0)


def _sb_prompt(q, kt, vt, bias):
    bsz, t, d = q.shape
    tq = min(ATT_TILE, t)
    assert t % tq == 0 and d == D_HEADS
    tri = (jnp.arange(tq)[:, None] >= jnp.arange(tq)[None, :]).astype(BF16)
    return pl.pallas_call(
        _sb_prompt_kernel,
        grid=(bsz, t // tq),
        in_specs=[
            pl.BlockSpec(memory_space=pltpu.SMEM),
            pl.BlockSpec((tq, tq), lambda b, i: (0, 0)),
            pl.BlockSpec((None, tq, d), lambda b, i: (b, i, 0)),
            pl.BlockSpec((None, d, t), lambda b, i: (b, 0, 0)),
            pl.BlockSpec((None, d, t), lambda b, i: (b, 0, 0)),
        ],
        out_specs=pl.BlockSpec((None, tq, d), lambda b, i: (b, i, 0)),
        out_shape=jax.ShapeDtypeStruct((bsz, t, d), F32),
        scratch_shapes=[pltpu.VMEM((N_HEADS, tq, 2 * HEAD_DIM), BF16), pltpu.VMEM((N_HEADS, tq, tq), F32),
                        pltpu.VMEM((N_HEADS, tq, 1), F32), pltpu.VMEM((N_HEADS, tq, 1), F32)],
        compiler_params=_params("parallel", "arbitrary"),
    )(bias, tri, q, kt, vt)


def _sb_paged_kernel(pt_ref, q_ref, knew_ref, vnew_ref, bias_ref, *refs, pages_per_step, past_len):
    del pt_ref
    pp = pages_per_step
    k_refs, v_refs = refs[:pp], refs[pp:2 * pp]
    o_ref, qcol_ref, acc_ref, carry_ref, w_ref = refs[2 * pp:]
    j = pl.program_id(1)
    d, page = k_refs[0].shape
    rows = pp * N_HEADS
    scale = HEAD_DIM ** -0.5

    @pl.when(j == 0)
    def _():
        q = q_ref[...]
        qcol_ref[...] = jnp.transpose(jnp.broadcast_to(q, (page, d)))
        assert knew_ref.shape[0] == 1
        new_idx = jnp.zeros((N_HEADS, page), jnp.int32)
        mask = (past_len + new_idx) < (past_len + new_idx)
        own = (lax.broadcasted_iota(jnp.int32, (N_HEADS, d), 0)
               == lax.broadcasted_iota(jnp.int32, (N_HEADS, d), 1) // HEAD_DIM)
        z = jnp.sum(jnp.where(own, q * knew_ref[...], 0.0), axis=-1, keepdims=True) * scale
        head_bias = jnp.concatenate([bias_ref[pl.ds(h * pp, 1), :] for h in range(N_HEADS)], axis=0)
        z = (z + head_bias) * LOG2_E
        carry_ref[...] = jnp.where(mask, _log2_one_minus_sigmoid(z), 0.0)
        w = jnp.where(mask, jnp.exp2(z), 0.0)
        vnew_col = jnp.transpose(jnp.broadcast_to(vnew_ref[...], (page, d)))
        acc_ref[...] = (vnew_col.reshape(N_HEADS, HEAD_DIM, page) * w[:, None, :]).reshape(d, page)

    z_rows = []
    for h in range(N_HEADS):
        feats = pl.ds(h * HEAD_DIM, HEAD_DIM)
        qh = qcol_ref[feats, :]
        z_rows += [jnp.sum(k_refs[i][feats, :] * qh, axis=0, keepdims=True) for i in range(pp)]
    z = jnp.concatenate(z_rows, axis=0)
    z = (z * scale + bias_ref[...]) * LOG2_E
    m = _log2_one_minus_sigmoid(z)
    ki = lax.broadcasted_iota(jnp.int32, (page, page), 0)
    si = lax.broadcasted_iota(jnp.int32, (page, page), 1)
    tri = (ki >= si).astype(BF16)
    ri = lax.broadcasted_iota(jnp.int32, (rows, rows), 0)
    ci = lax.broadcasted_iota(jnp.int32, (rows, rows), 1)
    later = ((ri // pp == ci // pp) & (ci > ri)).astype(BF16)
    csum = _dot_f32_by_01(m, tri)
    total = jnp.broadcast_to(csum[:, 0:1], csum.shape)
    after = _dot_01_by_f32(later, total)
    carry = carry_ref[...]
    carry_rows = jnp.concatenate([jnp.broadcast_to(carry[h:h + 1, :], (pp, page)) for h in range(N_HEADS)], axis=0)
    w_ref[...] = jnp.exp2(z + csum + after + carry_rows)
    for h in range(N_HEADS):
        feats = pl.ds(h * HEAD_DIM, HEAD_DIM)
        acc = acc_ref[feats, :]
        for i in range(pp):
            acc = acc + v_refs[i][feats, :] * w_ref[pl.ds(h * pp + i, 1), :]
        acc_ref[feats, :] = acc
    first = after + total
    carry_ref[...] = carry + jnp.concatenate([first[h * pp:h * pp + 1, :] for h in range(N_HEADS)], axis=0)

    @pl.when(j == pl.num_programs(1) - 1)
    def _():
        o_ref[...] = jnp.sum(jnp.transpose(acc_ref[...]), axis=0, keepdims=True)


def _sb_paged(q, k_new, v_new, cache_kt, cache_vt, layer, page_table, bias):
    bsz, d = q.shape
    n_pages = page_table.shape[1]
    page = cache_kt.shape[3]
    assert d == D_HEADS and page == LANES
    pp = min(PAGES_PER_STEP, n_pages)
    assert n_pages % pp == 0
    steps = n_pages // pp
    bias_tile = jnp.broadcast_to(bias.reshape(N_HEADS, 1, 1), (N_HEADS, pp, page)).reshape(N_HEADS * pp, page)

    def page_spec(i):
        return pl.BlockSpec((None, None, d, page),
                            lambda b, j, pt: (layer, pt[b, n_pages - pp * (j + 1) + i], 0, 0))

    seq = lambda: pl.BlockSpec((None, 1, d), lambda b, j, pt: (b, 0, 0))
    kernel = functools.partial(_sb_paged_kernel, pages_per_step=pp, past_len=n_pages * page)
    out = pl.pallas_call(
        kernel,
        grid_spec=pltpu.PrefetchScalarGridSpec(
            num_scalar_prefetch=1,
            grid=(bsz, steps),
            in_specs=[seq(), seq(), seq(), pl.BlockSpec(bias_tile.shape, lambda b, j, pt: (0, 0))]
            + [page_spec(i) for i in range(pp)] * 2,
            out_specs=seq(),
            scratch_shapes=[pltpu.VMEM((d, page), F32), pltpu.VMEM((d, page), F32),
                            pltpu.VMEM((N_HEADS, page), F32), pltpu.VMEM((N_HEADS * pp, page), F32)],
        ),
        out_shape=jax.ShapeDtypeStruct((bsz, 1, d), F32),
        compiler_params=_params("parallel", "arbitrary"),
    )(page_table, q.reshape(bsz, 1, d), k_new.reshape(bsz, 1, d), v_new.reshape(bsz, 1, d), bias_tile,
      *([cache_kt] * pp), *([cache_vt] * pp))
    return out.reshape(bsz, d)


def _outproj_rows(x_ref, h_ref, gl_ref, o_ref, gs_ref, w_ref, g_ref):
    d_lru = h_ref.shape[-1]
    y_lru = (h_ref[...] * jax.nn.silu(gl_ref[...])).astype(BF16)
    y_sb = (o_ref[...] * jax.nn.silu(gs_ref[...])).astype(BF16)
    m = _dot(y_lru, w_ref[pl.ds(0, d_lru), :]) + _dot(y_sb, w_ref[pl.ds(d_lru, w_ref.shape[0] - d_lru), :])
    n = m * lax.rsqrt(jnp.mean(m * m, axis=-1, keepdims=True) + RMS_EPS)
    return x_ref[...] + n * g_ref[...]


def _outproj_kernel(x_ref, h_ref, gl_ref, o_ref, gs_ref, w_ref, g_ref, y_ref):
    y_ref[...] = _outproj_rows(x_ref, h_ref, gl_ref, o_ref, gs_ref, w_ref, g_ref)


def _outproj(x2d, h, g_lru, o, g_sb, w_bf16, g):
    n, d = x2d.shape
    tm = min(ROW_TILE, n)
    assert n % tm == 0
    rows = lambda w: pl.BlockSpec((tm, w), lambda i: (i, 0))
    return pl.pallas_call(
        _outproj_kernel,
        grid=(n // tm,),
        in_specs=[rows(d), rows(h.shape[1]), rows(g_lru.shape[1]), rows(o.shape[1]), rows(g_sb.shape[1]),
                  pl.BlockSpec(w_bf16.shape, lambda i: (0, 0)),
                  pl.BlockSpec((1, d), lambda i: (0, 0))],
        out_specs=rows(d),
        out_shape=jax.ShapeDtypeStruct((n, d), F32),
        compiler_params=_params("parallel"),
    )(x2d, h, g_lru, o, g_sb, w_bf16, g.reshape(1, d))


def _gate_weights(w_r, w_i):
    nb, blk, _ = w_r.shape
    half_blocks = nb // 2

    def halves(w):
        eye = jnp.eye(half_blocks, dtype=w.dtype)
        w = w.reshape(2, half_blocks, blk, blk)
        return jnp.einsum('cnij,nm->cnimj', w, eye).reshape(2, half_blocks * blk, half_blocks * blk)

    return jnp.concatenate([halves(w_r), halves(w_i)], axis=-1).astype(BF16)


def _feature_major(a):
    lead = a.shape[:-3]
    n = len(lead)
    return jnp.transpose(a, (*range(n), n + 1, n + 2, n)).reshape(*lead, a.shape[-2] * a.shape[-1], a.shape[-3])


def _position_major(a, heads):
    lead = a.shape[:-2]
    n = len(lead)
    a = a.reshape(*lead, heads, a.shape[-2] // heads, a.shape[-1])
    return jnp.transpose(a, (*range(n), n + 2, n, n + 1))


def kernel(x_prompt, x_sample, cache_k, cache_v, state_lru_h, state_conv, page_table, norm_pre_g, norm_post_g, w_in, conv_w, conv_b, w_rgate, b_rgate, w_igate, b_igate, lru_lambda, sb_bias, w_out):
    bsz, seq, d_model = x_prompt.shape
    dec_b, dec_s, _ = x_sample.shape
    depth = w_in.shape[0]
    d_lru = lru_lambda.shape[1]
    assert dec_s == 1 and d_lru == D_HEADS
    width = d_lru
    ckt = _feature_major(cache_k)
    cvt = _feature_major(cache_v)
    col = lambda i: slice(i * width, (i + 1) * width)

    xp = x_prompt
    xs = x_sample.reshape(dec_b * dec_s, d_model)
    zero_conv = jnp.zeros((bsz, CONV_W - 1, d_lru), F32)
    zero_h = jnp.zeros((bsz, d_lru), F32)
    kp_l, vp_l, hp_l, cp_l, ks_l, vs_l, hs_l, cs_l = ([] for _ in range(8))
    prev = None
    for l in range(depth):
        w_l = w_in[l].astype(BF16)
        w_ugqg = jnp.concatenate([w_l[:, col(0)], w_l[:, col(1)], w_l[:, col(2)], w_l[:, col(5)]], axis=1)
        w_kv_t = jnp.transpose(w_l[:, 3 * width:5 * width])
        w_out_l = w_out[l].astype(BF16)
        wg = _gate_weights(w_rgate[l], w_igate[l])
        lru_p = (conv_w[l], conv_b[l], wg, b_rgate[l], b_igate[l], lru_lambda[l])

        outs = _inproj_lru(xp, norm_pre_g[l], w_ugqg, w_kv_t, zero_conv, zero_h, *lru_p, prev=prev)
        if prev is not None:
            xp, outs = outs[0], outs[1:]
        g_lru, q, g_sb, kt, vt, h, h_last, buf = outs
        o = _sb_prompt(q, kt, vt, sb_bias[l])
        prev = (h, g_lru, o, g_sb, w_out_l, norm_post_g[l])
        kp_l.append(kt)
        vp_l.append(vt)
        hp_l.append(h_last.reshape(bsz, d_lru))
        cp_l.append(buf)

        u, g_lru, q, k, v, g_sb = _inproj(xs, norm_pre_g[l], w_l, width)
        h, buf_t = _lru_step(u, jnp.swapaxes(state_conv[l], 0, 1), state_lru_h[l], *lru_p)
        o = _sb_paged(q, k, v, ckt, cvt, l, page_table, sb_bias[l])
        xs = _outproj(xs, h, g_lru, o, g_sb, w_out_l, norm_post_g[l])
        ks_l.append(k.reshape(dec_b, dec_s, N_HEADS, HEAD_DIM))
        vs_l.append(v.reshape(dec_b, dec_s, N_HEADS, HEAD_DIM))
        hs_l.append(h)
        cs_l.append(buf_t)

    flat = lambda a: a.reshape(bsz * seq, a.shape[-1])
    xp = _outproj(flat(xp), *(flat(a) for a in prev[:4]), *prev[4:]).reshape(bsz, seq, d_model)
    return (xp, xs.reshape(dec_b, dec_s, d_model),
            _position_major(jnp.stack(kp_l), N_HEADS), _position_major(jnp.stack(vp_l), N_HEADS),
            jnp.stack(hp_l), jnp.stack(cp_l),
            jnp.stack(ks_l), jnp.stack(vs_l), jnp.stack(hs_l), jnp.swapaxes(jnp.stack(cs_l), 1, 2))
```

```python
import functools

import jax
import jax.numpy as jnp
from jax import lax
from jax.experimental import pallas as pl
from jax.experimental.pallas import tpu as pltpu

F32 = jnp.float32
BF16 = jnp.bfloat16

RMS_EPS = 1e-6
LRU_C = 8.0
CONV_W = 4
N_HEADS = 8
HEAD_DIM = 64
D_HEADS = N_HEADS * HEAD_DIM

LANES = 128
SUBLANES = 8
VMEM_LIMIT_BYTES = 56 * 1024 * 1024

ROW_TILE = 1024
PROMPT_TILE = 512
ATT_TILE = 256
PAGES_PER_STEP = 32
SPLIT_PARTS = 3


def _params(*semantics):
    return pltpu.CompilerParams(dimension_semantics=semantics, vmem_limit_bytes=VMEM_LIMIT_BYTES)


def _softplus(z):
    return jnp.maximum(z, 0.0) + jnp.log1p(jnp.exp(-jnp.abs(z)))


LOG2_E = 1.4426950408889634


def _log2_one_minus_sigmoid(z2):
    n2 = -z2
    return jnp.minimum(n2, 0.0) - jnp.log2(1.0 + jnp.exp2(jnp.minimum(z2, n2)))


def _dot(a, b):
    return jnp.dot(a, b, preferred_element_type=F32)


def _dot_nt(a, b):
    return lax.dot_general(a, b, (((1,), (1,)), ((), ())), preferred_element_type=F32)


def _split_bf16(x, parts):
    pieces, r = [], x
    for _ in range(parts):
        p = r.astype(BF16)
        pieces.append(p)
        r = r - p.astype(F32)
    return pieces


def _dot_f32_by_01(x, w01, parts=SPLIT_PARTS):
    return sum(_dot(p, w01) for p in _split_bf16(x, parts))


def _dot_01_by_f32(w01, x, parts=SPLIT_PARTS):
    return sum(_dot(w01, p) for p in _split_bf16(x, parts))


def _sigmoid(x):
    return 0.5 * (jnp.tanh(0.5 * x) + 1.0)


def _rms_norm_bf16(x, g):
    y = x * lax.rsqrt(jnp.mean(x * x, axis=-1, keepdims=True) + RMS_EPS)
    return (y * g).astype(BF16)


def _inproj_kernel(x_ref, g_ref, w_ref, *out_refs):
    xn = _rms_norm_bf16(x_ref[...], g_ref[...])
    for i, o_ref in enumerate(out_refs):
        width = o_ref.shape[-1]
        o_ref[...] = _dot(xn, w_ref[:, i * width:(i + 1) * width])


def _inproj(x2d, g, w_bf16, width):
    n, d = x2d.shape
    n_out = w_bf16.shape[1] // width
    return pl.pallas_call(
        _inproj_kernel,
        out_shape=[jax.ShapeDtypeStruct((n, width), F32)] * n_out,
        compiler_params=pltpu.CompilerParams(vmem_limit_bytes=VMEM_LIMIT_BYTES),
    )(x2d, g.reshape(1, d), w_bf16)


def _lru_gate_logits(xc, wg_ref):
    half = xc.shape[-1] // 2
    xb = xc.astype(BF16)
    g0 = _dot(xb[:, :half], wg_ref[0])
    g1 = _dot(xb[:, half:], wg_ref[1])
    return (jnp.concatenate([g0[:, :half], g1[:, :half]], axis=-1),
            jnp.concatenate([g0[:, half:], g1[:, half:]], axis=-1))


def _lru_coefficients(xc, r_logit, i_logit, br, bi, lam):
    r = _sigmoid(r_logit + br)
    i = _sigmoid(i_logit + bi)
    log_a = LRU_C * r * (-_softplus(-lam))
    a = jnp.exp(log_a)
    b = jnp.sqrt(-jnp.tanh(log_a) * (a * a + 1.0)) * (i * xc)
    return a, b


def _inproj_lru_kernel(*refs, after_outproj, n_stacked):
    if after_outproj:
        x = _outproj_rows(*refs[:7])
        refs = refs[7:]
    else:
        x = refs[0][...]
        refs = refs[1:]
    g_ref, w_ref, wt_ref, conv0_ref, h0_ref, cw_ref, cb_ref, wg_ref, br_ref, bi_ref, lam_ref = refs[:11]
    earlier_k, earlier_v, refs = refs[11:11 + n_stacked], refs[11 + n_stacked:11 + 2 * n_stacked], refs[11 + 2 * n_stacked:]
    if after_outproj:
        refs[0][...] = x
        refs = refs[1:]
    gl_ref, q_ref, gs_ref, kt_ref, vt_ref, h_ref, hlast_ref, buf_ref, ext_ref, a_ref, b_ref, hc_ref = refs
    for stack_ref, earlier in ((kt_ref, earlier_k), (vt_ref, earlier_v)):
        for slot, e_ref in enumerate(earlier):
            stack_ref[slot] = e_ref[...]
    if n_stacked:
        kt_ref, vt_ref = kt_ref.at[n_stacked], vt_ref.at[n_stacked]
    tc, d = h_ref.shape
    pad = SUBLANES
    hist = CONV_W - 1

    @pl.when(pl.program_id(1) == 0)
    def _():
        hc_ref[...] = h0_ref[...]
        ext_ref[pl.ds(pad - hist, hist), :] = conv0_ref[...]

    xn = _rms_norm_bf16(x, g_ref[...])
    ext_ref[pl.ds(pad, tc), :] = _dot(xn, w_ref[:, 0:d])
    for i, o_ref in enumerate((gl_ref, q_ref, gs_ref)):
        o_ref[...] = _dot(xn, w_ref[:, (i + 1) * d:(i + 2) * d])
    for i, o_ref in enumerate((kt_ref, vt_ref)):
        o_ref[...] = _dot_nt(wt_ref[pl.ds(i * d, d), :], xn)

    xc = cb_ref[...] + ext_ref[pl.ds(pad - hist, tc), :] * cw_ref[pl.ds(0, 1), :]
    for j in range(1, CONV_W):
        xc = xc + ext_ref[pl.ds(pad - hist + j, tc), :] * cw_ref[pl.ds(j, 1), :]
    tail = ext_ref[pl.ds(pad + tc - hist, hist), :]
    ext_ref[pl.ds(pad - hist, hist), :] = tail
    buf_ref[...] = tail
    a, b = _lru_coefficients(xc, *_lru_gate_logits(xc, wg_ref), br_ref[...], bi_ref[...], lam_ref[...])

    groups = tc // SUBLANES
    ga = a.reshape(groups, SUBLANES, d)
    gb = b.reshape(groups, SUBLANES, d)
    step_in_group = lax.broadcasted_iota(jnp.int32, ga.shape, 1)
    for k in (1 << e for e in range(SUBLANES.bit_length() - 1)):
        has_prev = step_in_group >= k
        gb = jnp.where(has_prev, ga * pltpu.roll(gb, k, axis=1) + gb, gb)
        ga = jnp.where(has_prev, ga * pltpu.roll(ga, k, axis=1), ga)
    a_ref[...] = ga.reshape(tc, d)
    b_ref[...] = gb.reshape(tc, d)

    def group(i, h):
        rows = pl.ds(pl.multiple_of(i * SUBLANES, SUBLANES), SUBLANES)
        hg = a_ref[rows, :] * h + b_ref[rows, :]
        h_ref[rows, :] = hg
        return hg[SUBLANES - 1:SUBLANES, :]

    h = lax.fori_loop(0, groups, group, hc_ref[...], unroll=4)
    hc_ref[...] = h
    hlast_ref[...] = h


def _inproj_lru(x, g, w_ugqg, w_kv_t, conv0, h0, cw, cb, wg, br, bi, lam, prev=None, earlier_kv=None):
    bsz, t, d = x.shape
    width = lam.shape[-1]
    n_stacked = 0 if earlier_kv is None else len(earlier_kv[0])
    tm = min(PROMPT_TILE // 2 if n_stacked else PROMPT_TILE, t)
    hist = CONV_W - 1
    assert t % tm == 0 and tm >= hist and tm % SUBLANES == 0
    full = lambda a: pl.BlockSpec(a.shape, lambda b, i: (0,) * a.ndim)
    vec = lambda n: pl.BlockSpec((1, n), lambda b, i: (0, 0))
    rows = lambda n: pl.BlockSpec((None, tm, n), lambda b, i: (b, i, 0))
    cols = lambda: pl.BlockSpec((None, width, tm), lambda b, i: (b, 0, i))
    per_seq = lambda n: pl.BlockSpec((None, n, width), lambda b, i: (b, 0, 0))
    seq_state = lambda n: jax.ShapeDtypeStruct((bsz, n, width), F32)

    operands, in_specs = [x], [rows(d)]
    out_specs, out_shape = [], []
    if prev is not None:
        h_p, gl_p, o_p, gs_p, w_out, g_post = prev
        operands += [h_p, gl_p, o_p, gs_p, w_out, g_post.reshape(1, d)]
        in_specs += [rows(width)] * 4 + [full(w_out), vec(d)]
        out_specs, out_shape = [rows(d)], [jax.ShapeDtypeStruct((bsz, t, d), F32)]
    operands += [g.reshape(1, d), w_ugqg, w_kv_t, conv0, h0.reshape(bsz, 1, width), cw, cb.reshape(1, width), wg,
                 br.reshape(1, width), bi.reshape(1, width), lam.reshape(1, width)]
    in_specs += [vec(d), full(w_ugqg), full(w_kv_t), per_seq(hist), per_seq(1), full(cw), vec(width), full(wg),
                 vec(width), vec(width), vec(width)]
    kv_spec, kv_shape = cols(), jax.ShapeDtypeStruct((bsz, width, t), F32)
    if n_stacked:
        operands += [*earlier_kv[0], *earlier_kv[1]]
        in_specs += [cols()] * (2 * n_stacked)
        kv_spec = pl.BlockSpec((n_stacked + 1, None, width, tm), lambda b, i: (0, b, 0, i))
        kv_shape = jax.ShapeDtypeStruct((n_stacked + 1, bsz, width, t), F32)
    out_specs += [rows(width)] * 3 + [kv_spec, kv_spec, rows(width), per_seq(1), per_seq(hist)]
    out_shape += [seq_state(t)] * 3 + [kv_shape] * 2 + [seq_state(t), seq_state(1), seq_state(hist)]
    return pl.pallas_call(
        functools.partial(_inproj_lru_kernel, after_outproj=prev is not None, n_stacked=n_stacked),
        grid=(bsz, t // tm),
        in_specs=in_specs,
        out_specs=out_specs,
        out_shape=out_shape,
        scratch_shapes=[
            pltpu.VMEM((tm + SUBLANES, width), F32),
            pltpu.VMEM((tm, width), F32),
            pltpu.VMEM((tm, width), F32),
            pltpu.VMEM((1, width), F32),
        ],
        compiler_params=_params("parallel", "arbitrary"),
    )(*operands)


def _lru_step_kernel(u_ref, conv_ref, h0_ref, cw_ref, cb_ref, wg_ref, br_ref, bi_ref, lam_ref,
                     h_ref, buf_ref):
    u = u_ref[...]
    xc = cb_ref[...] + u * cw_ref[pl.ds(CONV_W - 1, 1), :]
    for j in range(CONV_W - 1):
        xc = xc + conv_ref[j] * cw_ref[pl.ds(j, 1), :]
    a, b = _lru_coefficients(xc, *_lru_gate_logits(xc, wg_ref), br_ref[...], bi_ref[...], lam_ref[...])
    h_ref[...] = a * h0_ref[...] + b
    for j in range(CONV_W - 2):
        buf_ref[j] = conv_ref[j + 1]
    buf_ref[CONV_W - 2] = u


def _lru_step(u, conv_t, h0, cw, cb, wg, br, bi, lam):
    bsz, d = u.shape
    return pl.pallas_call(
        _lru_step_kernel,
        out_shape=[jax.ShapeDtypeStruct((bsz, d), F32), jax.ShapeDtypeStruct(conv_t.shape, F32)],
        compiler_params=pltpu.CompilerParams(vmem_limit_bytes=VMEM_LIMIT_BYTES),
    )(u, conv_t, h0, cw, cb.reshape(1, d), wg, br.reshape(1, d), bi.reshape(1, d), lam.reshape(1, d))


def _sb_prompt_kernel(bias_ref, tri_ref, q_ref, kt_ref, vt_ref, o_ref, qm_ref, pre_ref, total_ref, carry_ref):
    tq = q_ref.shape[0]
    tk = tri_ref.shape[0]
    qi = pl.program_id(1)
    pair = 2 * HEAD_DIM
    scale = HEAD_DIM ** -0.5

    lane_head = lax.broadcasted_iota(jnp.int32, (tq, pair), 1) // HEAD_DIM
    for h in range(N_HEADS):
        qp = q_ref[:, pl.ds((h // 2) * pair, pair)] * scale
        qm_ref[h] = jnp.where(lane_head == h % 2, qp, 0.0).astype(BF16)
    row_head = lax.broadcasted_iota(jnp.int32, (pair, tk), 0) // HEAD_DIM
    visible = (lax.broadcasted_iota(jnp.int32, (tq, tk), 1) < lax.broadcasted_iota(jnp.int32, (tq, tk), 0))

    def keys_of(j):
        return pl.ds(pl.multiple_of(j * tk, tk), tk)

    def front(j, diagonal):
        for p in range(N_HEADS // 2):
            ktp = kt_ref[pl.ds(p * pair, pair), keys_of(j)].astype(BF16)
            for s in range(2):
                h = 2 * p + s
                z = (_dot(qm_ref[h], ktp) + bias_ref[h]) * LOG2_E
                m = _log2_one_minus_sigmoid(z)
                if diagonal:
                    m = jnp.where(visible, m, 0.0)
                csum = _dot(m.astype(BF16), tri_ref[...])
                pre = z + csum
                pre_ref[h] = jnp.where(visible, pre, -jnp.inf) if diagonal else pre
                total_ref[h] = csum[:, 0:1]

    def back(j):
        for p in range(N_HEADS // 2):
            feats = pl.ds(p * pair, pair)
            vtp = vt_ref[feats, keys_of(j)].astype(BF16)
            out = None
            for s in range(2):
                h = 2 * p + s
                carry = carry_ref[h]
                w = jnp.exp2(pre_ref[h] + carry)
                pv = _dot_nt(w.astype(BF16), jnp.where(row_head == s, vtp, jnp.zeros_like(vtp)))
                out = pv if out is None else out + pv
                carry_ref[h] = carry + total_ref[h]
            o_ref[:, feats] += out

    o_ref[...] = jnp.zeros(o_ref.shape, F32)
    carry_ref[...] = jnp.zeros(carry_ref.shape, F32)
    front(qi, True)

    def trip(i, c):
        back(qi - i + 1)
        front(qi - i, False)
        return c

    lax.fori_loop(1, qi + 1, trip, 0)
    back(0)


def _sb_prompt(q, kt, vt, bias, layer=None):
    bsz, t, d = q.shape
    if layer is None:
        kv_spec = pl.BlockSpec((None, d, t), lambda b, i: (b, 0, 0))
    else:
        kv_spec = pl.BlockSpec((None, None, d, t), lambda b, i: (layer, b, 0, 0))
    tq = min(ATT_TILE, t)
    assert t % tq == 0 and d == D_HEADS
    tri = (jnp.arange(tq)[:, None] >= jnp.arange(tq)[None, :]).astype(BF16)
    return pl.pallas_call(
        _sb_prompt_kernel,
        grid=(bsz, t // tq),
        in_specs=[
            pl.BlockSpec(memory_space=pltpu.SMEM),
            pl.BlockSpec((tq, tq), lambda b, i: (0, 0)),
            pl.BlockSpec((None, tq, d), lambda b, i: (b, i, 0)),
            kv_spec,
            kv_spec,
        ],
        out_specs=pl.BlockSpec((None, tq, d), lambda b, i: (b, i, 0)),
        out_shape=jax.ShapeDtypeStruct((bsz, t, d), F32),
        scratch_shapes=[pltpu.VMEM((N_HEADS, tq, 2 * HEAD_DIM), BF16), pltpu.VMEM((N_HEADS, tq, tq), F32),
                        pltpu.VMEM((N_HEADS, tq, 1), F32), pltpu.VMEM((N_HEADS, tq, 1), F32)],
        compiler_params=_params("parallel", "arbitrary"),
    )(bias, tri, q, kt, vt)


def _sb_paged_kernel(pt_ref, q_ref, knew_ref, vnew_ref, bias_ref, *refs, pages_per_step, past_len):
    del pt_ref
    pp = pages_per_step
    k_refs, v_refs = refs[:pp], refs[pp:2 * pp]
    o_ref, qcol_ref, acc_ref, carry_ref, w_ref = refs[2 * pp:]
    j = pl.program_id(1)
    d, page = k_refs[0].shape
    rows = pp * N_HEADS
    scale = HEAD_DIM ** -0.5

    @pl.when(j == 0)
    def _():
        q = q_ref[...]
        qcol_ref[...] = jnp.transpose(jnp.broadcast_to(q, (page, d)))
        assert knew_ref.shape[0] == 1
        new_idx = jnp.zeros((N_HEADS, page), jnp.int32)
        mask = (past_len + new_idx) < (past_len + new_idx)
        own = (lax.broadcasted_iota(jnp.int32, (N_HEADS, d), 0)
               == lax.broadcasted_iota(jnp.int32, (N_HEADS, d), 1) // HEAD_DIM)
        z = jnp.sum(jnp.where(own, q * knew_ref[...], 0.0), axis=-1, keepdims=True) * scale
        head_bias = jnp.concatenate([bias_ref[pl.ds(h * pp, 1), :] for h in range(N_HEADS)], axis=0)
        z = (z + head_bias) * LOG2_E
        carry_ref[...] = jnp.where(mask, _log2_one_minus_sigmoid(z), 0.0)
        w = jnp.where(mask, jnp.exp2(z), 0.0)
        vnew_col = jnp.transpose(jnp.broadcast_to(vnew_ref[...], (page, d)))
        acc_ref[...] = (vnew_col.reshape(N_HEADS, HEAD_DIM, page) * w[:, None, :]).reshape(d, page)

    z_rows = []
    for h in range(N_HEADS):
        feats = pl.ds(h * HEAD_DIM, HEAD_DIM)
        qh = qcol_ref[feats, :]
        z_rows += [jnp.sum(k_refs[i][feats, :] * qh, axis=0, keepdims=True) for i in range(pp)]
    z = jnp.concatenate(z_rows, axis=0)
    z = (z * scale + bias_ref[...]) * LOG2_E
    m = _log2_one_minus_sigmoid(z)
    ki = lax.broadcasted_iota(jnp.int32, (page, page), 0)
    si = lax.broadcasted_iota(jnp.int32, (page, page), 1)
    tri = (ki >= si).astype(BF16)
    ri = lax.broadcasted_iota(jnp.int32, (rows, rows), 0)
    ci = lax.broadcasted_iota(jnp.int32, (rows, rows), 1)
    later = ((ri // pp == ci // pp) & (ci > ri)).astype(BF16)
    csum = _dot_f32_by_01(m, tri)
    total = jnp.broadcast_to(csum[:, 0:1], csum.shape)
    after = _dot_01_by_f32(later, total)
    carry = carry_ref[...]
    carry_rows = jnp.concatenate([jnp.broadcast_to(carry[h:h + 1, :], (pp, page)) for h in range(N_HEADS)], axis=0)
    w_ref[...] = jnp.exp2(z + csum + after + carry_rows)
    for h in range(N_HEADS):
        feats = pl.ds(h * HEAD_DIM, HEAD_DIM)
        acc = acc_ref[feats, :]
        for i in range(pp):
            acc = acc + v_refs[i][feats, :] * w_ref[pl.ds(h * pp + i, 1), :]
        acc_ref[feats, :] = acc
    first = after + total
    carry_ref[...] = carry + jnp.concatenate([first[h * pp:h * pp + 1, :] for h in range(N_HEADS)], axis=0)

    @pl.when(j == pl.num_programs(1) - 1)
    def _():
        o_ref[...] = jnp.sum(jnp.transpose(acc_ref[...]), axis=0, keepdims=True)


def _sb_paged(q, k_new, v_new, cache_kt, cache_vt, layer, page_table, bias):
    bsz, d = q.shape
    n_pages = page_table.shape[1]
    page = cache_kt.shape[3]
    assert d == D_HEADS and page == LANES
    pp = min(PAGES_PER_STEP, n_pages)
    assert n_pages % pp == 0
    steps = n_pages // pp
    bias_tile = jnp.broadcast_to(bias.reshape(N_HEADS, 1, 1), (N_HEADS, pp, page)).reshape(N_HEADS * pp, page)

    def page_spec(i):
        return pl.BlockSpec((None, None, d, page),
                            lambda b, j, pt: (layer, pt[b, n_pages - pp * (j + 1) + i], 0, 0))

    seq = lambda: pl.BlockSpec((None, 1, d), lambda b, j, pt: (b, 0, 0))
    kernel = functools.partial(_sb_paged_kernel, pages_per_step=pp, past_len=n_pages * page)
    out = pl.pallas_call(
        kernel,
        grid_spec=pltpu.PrefetchScalarGridSpec(
            num_scalar_prefetch=1,
            grid=(bsz, steps),
            in_specs=[seq(), seq(), seq(), pl.BlockSpec(bias_tile.shape, lambda b, j, pt: (0, 0))]
            + [page_spec(i) for i in range(pp)] * 2,
            out_specs=seq(),
            scratch_shapes=[pltpu.VMEM((d, page), F32), pltpu.VMEM((d, page), F32),
                            pltpu.VMEM((N_HEADS, page), F32), pltpu.VMEM((N_HEADS * pp, page), F32)],
        ),
        out_shape=jax.ShapeDtypeStruct((bsz, 1, d), F32),
        compiler_params=_params("parallel", "arbitrary"),
    )(page_table, q.reshape(bsz, 1, d), k_new.reshape(bsz, 1, d), v_new.reshape(bsz, 1, d), bias_tile,
      *([cache_kt] * pp), *([cache_vt] * pp))
    return out.reshape(bsz, d)


def _outproj_rows(x_ref, h_ref, gl_ref, o_ref, gs_ref, w_ref, g_ref):
    d_lru = h_ref.shape[-1]
    y_lru = (h_ref[...] * jax.nn.silu(gl_ref[...])).astype(BF16)
    y_sb = (o_ref[...] * jax.nn.silu(gs_ref[...])).astype(BF16)
    m = _dot(y_lru, w_ref[pl.ds(0, d_lru), :]) + _dot(y_sb, w_ref[pl.ds(d_lru, w_ref.shape[0] - d_lru), :])
    n = m * lax.rsqrt(jnp.mean(m * m, axis=-1, keepdims=True) + RMS_EPS)
    return x_ref[...] + n * g_ref[...]


def _outproj_kernel(x_ref, h_ref, gl_ref, o_ref, gs_ref, w_ref, g_ref, y_ref):
    y_ref[...] = _outproj_rows(x_ref, h_ref, gl_ref, o_ref, gs_ref, w_ref, g_ref)


def _outproj(x2d, h, g_lru, o, g_sb, w_bf16, g):
    n, d = x2d.shape
    tm = min(ROW_TILE, n)
    assert n % tm == 0
    rows = lambda w: pl.BlockSpec((tm, w), lambda i: (i, 0))
    return pl.pallas_call(
        _outproj_kernel,
        grid=(n // tm,),
        in_specs=[rows(d), rows(h.shape[1]), rows(g_lru.shape[1]), rows(o.shape[1]), rows(g_sb.shape[1]),
                  pl.BlockSpec(w_bf16.shape, lambda i: (0, 0)),
                  pl.BlockSpec((1, d), lambda i: (0, 0))],
        out_specs=rows(d),
        out_shape=jax.ShapeDtypeStruct((n, d), F32),
        compiler_params=_params("parallel"),
    )(x2d, h, g_lru, o, g_sb, w_bf16, g.reshape(1, d))


def _gate_weights(w_r, w_i):
    nb, blk, _ = w_r.shape
    half_blocks = nb // 2

    def halves(w):
        eye = jnp.eye(half_blocks, dtype=w.dtype)
        w = w.reshape(2, half_blocks, blk, blk)
        return jnp.einsum('cnij,nm->cnimj', w, eye).reshape(2, half_blocks * blk, half_blocks * blk)

    return jnp.concatenate([halves(w_r), halves(w_i)], axis=-1).astype(BF16)


def _feature_major(a):
    lead = a.shape[:-3]
    n = len(lead)
    return jnp.transpose(a, (*range(n), n + 1, n + 2, n)).reshape(*lead, a.shape[-2] * a.shape[-1], a.shape[-3])


def _position_major(a, heads):
    lead = a.shape[:-2]
    n = len(lead)
    a = a.reshape(*lead, heads, a.shape[-2] // heads, a.shape[-1])
    return jnp.transpose(a, (*range(n), n + 2, n, n + 1))


def kernel(x_prompt, x_sample, cache_k, cache_v, state_lru_h, state_conv, page_table, norm_pre_g, norm_post_g, w_in, conv_w, conv_b, w_rgate, b_rgate, w_igate, b_igate, lru_lambda, sb_bias, w_out):
    bsz, seq, d_model = x_prompt.shape
    dec_b, dec_s, _ = x_sample.shape
    depth = w_in.shape[0]
    d_lru = lru_lambda.shape[1]
    assert dec_s == 1 and d_lru == D_HEADS
    width = d_lru
    ckt = _feature_major(cache_k)
    cvt = _feature_major(cache_v)
    col = lambda i: slice(i * width, (i + 1) * width)

    xp = x_prompt
    xs = x_sample.reshape(dec_b * dec_s, d_model)
    zero_conv = jnp.zeros((bsz, CONV_W - 1, d_lru), F32)
    zero_h = jnp.zeros((bsz, d_lru), F32)
    kp_l, vp_l, hp_l, cp_l, ks_l, vs_l, hs_l, cs_l = ([] for _ in range(8))
    prev = None
    for l in range(depth):
        w_l = w_in[l].astype(BF16)
        w_ugqg = jnp.concatenate([w_l[:, col(0)], w_l[:, col(1)], w_l[:, col(2)], w_l[:, col(5)]], axis=1)
        w_kv_t = jnp.transpose(w_l[:, 3 * width:5 * width])
        w_out_l = w_out[l].astype(BF16)
        wg = _gate_weights(w_rgate[l], w_igate[l])
        lru_p = (conv_w[l], conv_b[l], wg, b_rgate[l], b_igate[l], lru_lambda[l])

        last = l == depth - 1 and depth > 1
        outs = _inproj_lru(xp, norm_pre_g[l], w_ugqg, w_kv_t, zero_conv, zero_h, *lru_p, prev=prev,
                           earlier_kv=(kp_l, vp_l) if last else None)
        if prev is not None:
            xp, outs = outs[0], outs[1:]
        g_lru, q, g_sb, kt, vt, h, h_last, buf = outs
        o = _sb_prompt(q, kt, vt, sb_bias[l], layer=l if last else None)
        prev = (h, g_lru, o, g_sb, w_out_l, norm_post_g[l])
        kp_l.append(kt)
        vp_l.append(vt)
        hp_l.append(h_last.reshape(bsz, d_lru))
        cp_l.append(buf)

        u, g_lru, q, k, v, g_sb = _inproj(xs, norm_pre_g[l], w_l, width)
        h, buf_t = _lru_step(u, jnp.swapaxes(state_conv[l], 0, 1), state_lru_h[l], *lru_p)
        o = _sb_paged(q, k, v, ckt, cvt, l, page_table, sb_bias[l])
        xs = _outproj(xs, h, g_lru, o, g_sb, w_out_l, norm_post_g[l])
        ks_l.append(k.reshape(dec_b, dec_s, N_HEADS, HEAD_DIM))
        vs_l.append(v.reshape(dec_b, dec_s, N_HEADS, HEAD_DIM))
        hs_l.append(h)
        cs_l.append(buf_t)

    k_stack, v_stack = (kp_l[-1], vp_l[-1]) if depth > 1 else (kp_l[0][None], vp_l[0][None])
    flat = lambda a: a.reshape(bsz * seq, a.shape[-1])
    xp = _outproj(flat(xp), *(flat(a) for a in prev[:4]), *prev[4:]).reshape(bsz, seq, d_model)
    return (xp, xs.reshape(dec_b, dec_s, d_model),
            _position_major(k_stack, N_HEADS), _position_major(v_stack, N_HEADS),
            jnp.stack(hp_l), jnp.stack(cp_l),
            jnp.stack(ks_l), jnp.stack(vs_l), jnp.stack(hs_l), jnp.swapaxes(jnp.stack(cs_l), 1, 2))
```

```python
import functools

import jax
import jax.numpy as jnp
from jax import lax
from jax.experimental import pallas as pl
from jax.experimental.pallas import tpu as pltpu

F32 = jnp.float32
BF16 = jnp.bfloat16

RMS_EPS = 1e-6
LRU_C = 8.0
CONV_W = 4
N_HEADS = 8
HEAD_DIM = 64
D_HEADS = N_HEADS * HEAD_DIM

LANES = 128
SUBLANES = 8
VMEM_LIMIT_BYTES = 56 * 1024 * 1024

ROW_TILE = 1024
PROMPT_TILE = 512
ATT_TILE = 256
PAGES_PER_STEP = 32
SPLIT_PARTS = 3


def _params(*semantics):
    return pltpu.CompilerParams(dimension_semantics=semantics, vmem_limit_bytes=VMEM_LIMIT_BYTES)


def _softplus(z):
    return jnp.maximum(z, 0.0) + jnp.log1p(jnp.exp(-jnp.abs(z)))


LOG2_E = 1.4426950408889634


def _log2_one_minus_sigmoid(z2):
    n2 = -z2
    return jnp.minimum(n2, 0.0) - jnp.log2(1.0 + jnp.exp2(jnp.minimum(z2, n2)))


def _dot(a, b):
    return jnp.dot(a, b, preferred_element_type=F32)


def _dot_nt(a, b):
    return lax.dot_general(a, b, (((1,), (1,)), ((), ())), preferred_element_type=F32)


def _split_bf16(x, parts):
    pieces, r = [], x
    for _ in range(parts):
        p = r.astype(BF16)
        pieces.append(p)
        r = r - p.astype(F32)
    return pieces


def _dot_f32_by_01(x, w01, parts=SPLIT_PARTS):
    return sum(_dot(p, w01) for p in _split_bf16(x, parts))


def _dot_01_by_f32(w01, x, parts=SPLIT_PARTS):
    return sum(_dot(w01, p) for p in _split_bf16(x, parts))


def _sigmoid(x):
    return 0.5 * (jnp.tanh(0.5 * x) + 1.0)


def _rms_norm_bf16(x, g):
    y = x * lax.rsqrt(jnp.mean(x * x, axis=-1, keepdims=True) + RMS_EPS)
    return (y * g).astype(BF16)


def _inproj_kernel(x_ref, g_ref, w_ref, *out_refs):
    xn = _rms_norm_bf16(x_ref[...], g_ref[...])
    for i, o_ref in enumerate(out_refs):
        width = o_ref.shape[-1]
        o_ref[...] = _dot(xn, w_ref[:, i * width:(i + 1) * width])


def _inproj(x2d, g, w_bf16, width):
    n, d = x2d.shape
    n_out = w_bf16.shape[1] // width
    return pl.pallas_call(
        _inproj_kernel,
        out_shape=[jax.ShapeDtypeStruct((n, width), F32)] * n_out,
        compiler_params=pltpu.CompilerParams(vmem_limit_bytes=VMEM_LIMIT_BYTES),
    )(x2d, g.reshape(1, d), w_bf16)


def _lru_gate_logits(xc, wg_ref):
    half = xc.shape[-1] // 2
    xb = xc.astype(BF16)
    g0 = _dot(xb[:, :half], wg_ref[0])
    g1 = _dot(xb[:, half:], wg_ref[1])
    return (jnp.concatenate([g0[:, :half], g1[:, :half]], axis=-1),
            jnp.concatenate([g0[:, half:], g1[:, half:]], axis=-1))


def _lru_coefficients(xc, r_logit, i_logit, br, bi, lam):
    r = _sigmoid(r_logit + br)
    i = _sigmoid(i_logit + bi)
    log_a = LRU_C * r * (-_softplus(-lam))
    a = jnp.exp(log_a)
    b = jnp.sqrt(-jnp.tanh(log_a) * (a * a + 1.0)) * (i * xc)
    return a, b


def _inproj_lru_kernel(*refs, after_outproj):
    if after_outproj:
        x = _outproj_rows(*refs[:7])
        refs = refs[7:]
    else:
        x = refs[0][...]
        refs = refs[1:]
    g_ref, w_ref, wt_ref, conv0_ref, h0_ref, cw_ref, cb_ref, wg_ref, br_ref, bi_ref, lam_ref = refs[:11]
    refs = refs[11:]
    if after_outproj:
        refs[0][...] = x
        refs = refs[1:]
    gl_ref, q_ref, gs_ref, kt_ref, vt_ref, h_ref, hlast_ref, buf_ref, ext_ref, a_ref, b_ref, hc_ref = refs
    tc, d = h_ref.shape
    pad = SUBLANES
    hist = CONV_W - 1

    @pl.when(pl.program_id(1) == 0)
    def _():
        hc_ref[...] = h0_ref[...]
        ext_ref[pl.ds(pad - hist, hist), :] = conv0_ref[...]

    xn = _rms_norm_bf16(x, g_ref[...])
    ext_ref[pl.ds(pad, tc), :] = _dot(xn, w_ref[:, 0:d])
    for i, o_ref in enumerate((gl_ref, q_ref, gs_ref)):
        o_ref[...] = _dot(xn, w_ref[:, (i + 1) * d:(i + 2) * d])
    for i, o_ref in enumerate((kt_ref, vt_ref)):
        o_ref[...] = _dot_nt(wt_ref[pl.ds(i * d, d), :], xn)

    xc = cb_ref[...] + ext_ref[pl.ds(pad - hist, tc), :] * cw_ref[pl.ds(0, 1), :]
    for j in range(1, CONV_W):
        xc = xc + ext_ref[pl.ds(pad - hist + j, tc), :] * cw_ref[pl.ds(j, 1), :]
    tail = ext_ref[pl.ds(pad + tc - hist, hist), :]
    ext_ref[pl.ds(pad - hist, hist), :] = tail
    buf_ref[...] = tail
    a, b = _lru_coefficients(xc, *_lru_gate_logits(xc, wg_ref), br_ref[...], bi_ref[...], lam_ref[...])

    groups = tc // SUBLANES
    ga = a.reshape(groups, SUBLANES, d)
    gb = b.reshape(groups, SUBLANES, d)
    step_in_group = lax.broadcasted_iota(jnp.int32, ga.shape, 1)
    for k in (1 << e for e in range(SUBLANES.bit_length() - 1)):
        has_prev = step_in_group >= k
        gb = jnp.where(has_prev, ga * pltpu.roll(gb, k, axis=1) + gb, gb)
        ga = jnp.where(has_prev, ga * pltpu.roll(ga, k, axis=1), ga)
    a_ref[...] = ga.reshape(tc, d)
    b_ref[...] = gb.reshape(tc, d)

    def group(i, h):
        rows = pl.ds(pl.multiple_of(i * SUBLANES, SUBLANES), SUBLANES)
        hg = a_ref[rows, :] * h + b_ref[rows, :]
        h_ref[rows, :] = hg
        return hg[SUBLANES - 1:SUBLANES, :]

    h = lax.fori_loop(0, groups, group, hc_ref[...], unroll=4)
    hc_ref[...] = h
    hlast_ref[...] = h


def _inproj_lru(x, g, w_ugqg, w_kv_t, conv0, h0, cw, cb, wg, br, bi, lam, prev=None):
    bsz, t, d = x.shape
    width = lam.shape[-1]
    tm = min(PROMPT_TILE, t)
    hist = CONV_W - 1
    assert t % tm == 0 and tm >= hist and tm % SUBLANES == 0
    full = lambda a: pl.BlockSpec(a.shape, lambda b, i: (0,) * a.ndim)
    vec = lambda n: pl.BlockSpec((1, n), lambda b, i: (0, 0))
    rows = lambda n: pl.BlockSpec((None, tm, n), lambda b, i: (b, i, 0))
    cols = lambda: pl.BlockSpec((None, width, tm), lambda b, i: (b, 0, i))
    per_seq = lambda n: pl.BlockSpec((None, n, width), lambda b, i: (b, 0, 0))
    seq_state = lambda n: jax.ShapeDtypeStruct((bsz, n, width), F32)

    operands, in_specs = [x], [rows(d)]
    out_specs, out_shape = [], []
    if prev is not None:
        h_p, gl_p, o_p, gs_p, w_out, g_post = prev
        operands += [h_p, gl_p, o_p, gs_p, w_out, g_post.reshape(1, d)]
        in_specs += [rows(width)] * 4 + [full(w_out), vec(d)]
        out_specs, out_shape = [rows(d)], [jax.ShapeDtypeStruct((bsz, t, d), F32)]
    operands += [g.reshape(1, d), w_ugqg, w_kv_t, conv0, h0.reshape(bsz, 1, width), cw, cb.reshape(1, width), wg,
                 br.reshape(1, width), bi.reshape(1, width), lam.reshape(1, width)]
    in_specs += [vec(d), full(w_ugqg), full(w_kv_t), per_seq(hist), per_seq(1), full(cw), vec(width), full(wg),
                 vec(width), vec(width), vec(width)]
    out_specs += [rows(width)] * 3 + [cols(), cols(), rows(width), per_seq(1), per_seq(hist)]
    out_shape += [seq_state(t)] * 3 + [jax.ShapeDtypeStruct((bsz, width, t), F32)] * 2 + [
        seq_state(t), seq_state(1), seq_state(hist)]
    return pl.pallas_call(
        functools.partial(_inproj_lru_kernel, after_outproj=prev is not None),
        grid=(bsz, t // tm),
        in_specs=in_specs,
        out_specs=out_specs,
        out_shape=out_shape,
        scratch_shapes=[
            pltpu.VMEM((tm + SUBLANES, width), F32),
            pltpu.VMEM((tm, width), F32),
            pltpu.VMEM((tm, width), F32),
            pltpu.VMEM((1, width), F32),
        ],
        compiler_params=_params("parallel", "arbitrary"),
    )(*operands)


def _lru_step_kernel(u_ref, conv_ref, h0_ref, cw_ref, cb_ref, wg_ref, br_ref, bi_ref, lam_ref,
                     h_ref, buf_ref):
    u = u_ref[...]
    xc = cb_ref[...] + u * cw_ref[pl.ds(CONV_W - 1, 1), :]
    for j in range(CONV_W - 1):
        xc = xc + conv_ref[j] * cw_ref[pl.ds(j, 1), :]
    a, b = _lru_coefficients(xc, *_lru_gate_logits(xc, wg_ref), br_ref[...], bi_ref[...], lam_ref[...])
    h_ref[...] = a * h0_ref[...] + b
    for j in range(CONV_W - 2):
        buf_ref[j] = conv_ref[j + 1]
    buf_ref[CONV_W - 2] = u


def _lru_step(u, conv_t, h0, cw, cb, wg, br, bi, lam):
    bsz, d = u.shape
    return pl.pallas_call(
        _lru_step_kernel,
        out_shape=[jax.ShapeDtypeStruct((bsz, d), F32), jax.ShapeDtypeStruct(conv_t.shape, F32)],
        compiler_params=pltpu.CompilerParams(vmem_limit_bytes=VMEM_LIMIT_BYTES),
    )(u, conv_t, h0, cw, cb.reshape(1, d), wg, br.reshape(1, d), bi.reshape(1, d), lam.reshape(1, d))


def _sb_prompt_kernel(bias_ref, tri_ref, q_ref, kt_ref, vt_ref, *refs, n_stacked):
    tq = q_ref.shape[0]
    tk = tri_ref.shape[0]
    qi = pl.program_id(1)
    if n_stacked:
        earlier, refs = (refs[:n_stacked], refs[n_stacked:2 * n_stacked]), refs[2 * n_stacked:]
        o_ref, k_stack_ref, v_stack_ref, qm_ref, pre_ref, total_ref, carry_ref = refs
        own_cols = pl.ds(pl.multiple_of(qi * tq, tq), tq)
        for stack_ref, earlier_refs, own_ref in ((k_stack_ref, earlier[0], kt_ref), (v_stack_ref, earlier[1], vt_ref)):
            for slot, e_ref in enumerate(earlier_refs):
                stack_ref[slot] = e_ref[...]
            stack_ref[n_stacked] = own_ref[:, own_cols]
    else:
        o_ref, qm_ref, pre_ref, total_ref, carry_ref = refs
    pair = 2 * HEAD_DIM
    scale = HEAD_DIM ** -0.5

    lane_head = lax.broadcasted_iota(jnp.int32, (tq, pair), 1) // HEAD_DIM
    for h in range(N_HEADS):
        qp = q_ref[:, pl.ds((h // 2) * pair, pair)] * scale
        qm_ref[h] = jnp.where(lane_head == h % 2, qp, 0.0).astype(BF16)
    row_head = lax.broadcasted_iota(jnp.int32, (pair, tk), 0) // HEAD_DIM
    visible = (lax.broadcasted_iota(jnp.int32, (tq, tk), 1) < lax.broadcasted_iota(jnp.int32, (tq, tk), 0))

    def keys_of(j):
        return pl.ds(pl.multiple_of(j * tk, tk), tk)

    def front(j, diagonal):
        for p in range(N_HEADS // 2):
            ktp = kt_ref[pl.ds(p * pair, pair), keys_of(j)].astype(BF16)
            for s in range(2):
                h = 2 * p + s
                z = (_dot(qm_ref[h], ktp) + bias_ref[h]) * LOG2_E
                m = _log2_one_minus_sigmoid(z)
                if diagonal:
                    m = jnp.where(visible, m, 0.0)
                csum = _dot(m.astype(BF16), tri_ref[...])
                pre = z + csum
                pre_ref[h] = jnp.where(visible, pre, -jnp.inf) if diagonal else pre
                total_ref[h] = csum[:, 0:1]

    def back(j):
        for p in range(N_HEADS // 2):
            feats = pl.ds(p * pair, pair)
            vtp = vt_ref[feats, keys_of(j)].astype(BF16)
            out = None
            for s in range(2):
                h = 2 * p + s
                carry = carry_ref[h]
                w = jnp.exp2(pre_ref[h] + carry)
                pv = _dot_nt(w.astype(BF16), jnp.where(row_head == s, vtp, jnp.zeros_like(vtp)))
                out = pv if out is None else out + pv
                carry_ref[h] = carry + total_ref[h]
            o_ref[:, feats] += out

    o_ref[...] = jnp.zeros(o_ref.shape, F32)
    carry_ref[...] = jnp.zeros(carry_ref.shape, F32)
    front(qi, True)

    def trip(i, c):
        back(qi - i + 1)
        front(qi - i, False)
        return c

    lax.fori_loop(1, qi + 1, trip, 0)
    back(0)


def _sb_prompt(q, kt, vt, bias, earlier_kv=None):
    bsz, t, d = q.shape
    tq = min(ATT_TILE, t)
    assert t % tq == 0 and d == D_HEADS
    n_stacked = 0 if earlier_kv is None else len(earlier_kv[0])
    tri = (jnp.arange(tq)[:, None] >= jnp.arange(tq)[None, :]).astype(BF16)
    kv_spec = pl.BlockSpec((None, d, t), lambda b, i: (b, 0, 0))
    operands = [bias, tri, q, kt, vt]
    in_specs = [pl.BlockSpec(memory_space=pltpu.SMEM), pl.BlockSpec((tq, tq), lambda b, i: (0, 0)),
                pl.BlockSpec((None, tq, d), lambda b, i: (b, i, 0)), kv_spec, kv_spec]
    out_specs = [pl.BlockSpec((None, tq, d), lambda b, i: (b, i, 0))]
    out_shape = [jax.ShapeDtypeStruct((bsz, t, d), F32)]
    if n_stacked:
        operands += [*earlier_kv[0], *earlier_kv[1]]
        in_specs += [pl.BlockSpec((None, d, tq), lambda b, i: (b, 0, i))] * (2 * n_stacked)
        out_specs += [pl.BlockSpec((n_stacked + 1, None, d, tq), lambda b, i: (0, b, 0, i))] * 2
        out_shape += [jax.ShapeDtypeStruct((n_stacked + 1, bsz, d, t), F32)] * 2
    outs = pl.pallas_call(
        functools.partial(_sb_prompt_kernel, n_stacked=n_stacked),
        grid=(bsz, t // tq),
        in_specs=in_specs,
        out_specs=out_specs,
        out_shape=out_shape,
        scratch_shapes=[pltpu.VMEM((N_HEADS, tq, 2 * HEAD_DIM), BF16), pltpu.VMEM((N_HEADS, tq, tq), F32),
                        pltpu.VMEM((N_HEADS, tq, 1), F32), pltpu.VMEM((N_HEADS, tq, 1), F32)],
        compiler_params=_params("parallel", "arbitrary"),
    )(*operands)
    return outs if n_stacked else outs[0]


def _sb_paged_kernel(pt_ref, q_ref, knew_ref, vnew_ref, bias_ref, *refs, pages_per_step, past_len):
    del pt_ref
    pp = pages_per_step
    k_refs, v_refs = refs[:pp], refs[pp:2 * pp]
    o_ref, qcol_ref, acc_ref, carry_ref, w_ref = refs[2 * pp:]
    j = pl.program_id(1)
    d, page = k_refs[0].shape
    rows = pp * N_HEADS
    scale = HEAD_DIM ** -0.5

    @pl.when(j == 0)
    def _():
        q = q_ref[...]
        qcol_ref[...] = jnp.transpose(jnp.broadcast_to(q, (page, d)))
        assert knew_ref.shape[0] == 1
        new_idx = jnp.zeros((N_HEADS, page), jnp.int32)
        mask = (past_len + new_idx) < (past_len + new_idx)
        own = (lax.broadcasted_iota(jnp.int32, (N_HEADS, d), 0)
               == lax.broadcasted_iota(jnp.int32, (N_HEADS, d), 1) // HEAD_DIM)
        z = jnp.sum(jnp.where(own, q * knew_ref[...], 0.0), axis=-1, keepdims=True) * scale
        head_bias = jnp.concatenate([bias_ref[pl.ds(h * pp, 1), :] for h in range(N_HEADS)], axis=0)
        z = (z + head_bias) * LOG2_E
        carry_ref[...] = jnp.where(mask, _log2_one_minus_sigmoid(z), 0.0)
        w = jnp.where(mask, jnp.exp2(z), 0.0)
        vnew_col = jnp.transpose(jnp.broadcast_to(vnew_ref[...], (page, d)))
        acc_ref[...] = (vnew_col.reshape(N_HEADS, HEAD_DIM, page) * w[:, None, :]).reshape(d, page)

    z_rows = []
    for h in range(N_HEADS):
        feats = pl.ds(h * HEAD_DIM, HEAD_DIM)
        qh = qcol_ref[feats, :]
        z_rows += [jnp.sum(k_refs[i][feats, :] * qh, axis=0, keepdims=True) for i in range(pp)]
    z = jnp.concatenate(z_rows, axis=0)
    z = (z * scale + bias_ref[...]) * LOG2_E
    m = _log2_one_minus_sigmoid(z)
    ki = lax.broadcasted_iota(jnp.int32, (page, page), 0)
    si = lax.broadcasted_iota(jnp.int32, (page, page), 1)
    tri = (ki >= si).astype(BF16)
    ri = lax.broadcasted_iota(jnp.int32, (rows, rows), 0)
    ci = lax.broadcasted_iota(jnp.int32, (rows, rows), 1)
    later = ((ri // pp == ci // pp) & (ci > ri)).astype(BF16)
    csum = _dot_f32_by_01(m, tri)
    total = jnp.broadcast_to(csum[:, 0:1], csum.shape)
    after = _dot_01_by_f32(later, total)
    carry = carry_ref[...]
    carry_rows = jnp.concatenate([jnp.broadcast_to(carry[h:h + 1, :], (pp, page)) for h in range(N_HEADS)], axis=0)
    w_ref[...] = jnp.exp2(z + csum + after + carry_rows)
    for h in range(N_HEADS):
        feats = pl.ds(h * HEAD_DIM, HEAD_DIM)
        acc = acc_ref[feats, :]
        for i in range(pp):
            acc = acc + v_refs[i][feats, :] * w_ref[pl.ds(h * pp + i, 1), :]
        acc_ref[feats, :] = acc
    first = after + total
    carry_ref[...] = carry + jnp.concatenate([first[h * pp:h * pp + 1, :] for h in range(N_HEADS)], axis=0)

    @pl.when(j == pl.num_programs(1) - 1)
    def _():
        o_ref[...] = jnp.sum(jnp.transpose(acc_ref[...]), axis=0, keepdims=True)


def _sb_paged(q, k_new, v_new, cache_kt, cache_vt, layer, page_table, bias):
    bsz, d = q.shape
    n_pages = page_table.shape[1]
    page = cache_kt.shape[3]
    assert d == D_HEADS and page == LANES
    pp = min(PAGES_PER_STEP, n_pages)
    assert n_pages % pp == 0
    steps = n_pages // pp
    bias_tile = jnp.broadcast_to(bias.reshape(N_HEADS, 1, 1), (N_HEADS, pp, page)).reshape(N_HEADS * pp, page)

    def page_spec(i):
        return pl.BlockSpec((None, None, d, page),
                            lambda b, j, pt: (layer, pt[b, n_pages - pp * (j + 1) + i], 0, 0))

    seq = lambda: pl.BlockSpec((None, 1, d), lambda b, j, pt: (b, 0, 0))
    kernel = functools.partial(_sb_paged_kernel, pages_per_step=pp, past_len=n_pages * page)
    out = pl.pallas_call(
        kernel,
        grid_spec=pltpu.PrefetchScalarGridSpec(
            num_scalar_prefetch=1,
            grid=(bsz, steps),
            in_specs=[seq(), seq(), seq(), pl.BlockSpec(bias_tile.shape, lambda b, j, pt: (0, 0))]
            + [page_spec(i) for i in range(pp)] * 2,
            out_specs=seq(),
            scratch_shapes=[pltpu.VMEM((d, page), F32), pltpu.VMEM((d, page), F32),
                            pltpu.VMEM((N_HEADS, page), F32), pltpu.VMEM((N_HEADS * pp, page), F32)],
        ),
        out_shape=jax.ShapeDtypeStruct((bsz, 1, d), F32),
        compiler_params=_params("parallel", "arbitrary"),
    )(page_table, q.reshape(bsz, 1, d), k_new.reshape(bsz, 1, d), v_new.reshape(bsz, 1, d), bias_tile,
      *([cache_kt] * pp), *([cache_vt] * pp))
    return out.reshape(bsz, d)


def _outproj_rows(x_ref, h_ref, gl_ref, o_ref, gs_ref, w_ref, g_ref):
    d_lru = h_ref.shape[-1]
    y_lru = (h_ref[...] * jax.nn.silu(gl_ref[...])).astype(BF16)
    y_sb = (o_ref[...] * jax.nn.silu(gs_ref[...])).astype(BF16)
    m = _dot(y_lru, w_ref[pl.ds(0, d_lru), :]) + _dot(y_sb, w_ref[pl.ds(d_lru, w_ref.shape[0] - d_lru), :])
    n = m * lax.rsqrt(jnp.mean(m * m, axis=-1, keepdims=True) + RMS_EPS)
    return x_ref[...] + n * g_ref[...]


def _outproj_kernel(x_ref, h_ref, gl_ref, o_ref, gs_ref, w_ref, g_ref, y_ref):
    y_ref[...] = _outproj_rows(x_ref, h_ref, gl_ref, o_ref, gs_ref, w_ref, g_ref)


def _outproj(x2d, h, g_lru, o, g_sb, w_bf16, g):
    n, d = x2d.shape
    tm = min(ROW_TILE, n)
    assert n % tm == 0
    rows = lambda w: pl.BlockSpec((tm, w), lambda i: (i, 0))
    return pl.pallas_call(
        _outproj_kernel,
        grid=(n // tm,),
        in_specs=[rows(d), rows(h.shape[1]), rows(g_lru.shape[1]), rows(o.shape[1]), rows(g_sb.shape[1]),
                  pl.BlockSpec(w_bf16.shape, lambda i: (0, 0)),
                  pl.BlockSpec((1, d), lambda i: (0, 0))],
        out_specs=rows(d),
        out_shape=jax.ShapeDtypeStruct((n, d), F32),
        compiler_params=_params("parallel"),
    )(x2d, h, g_lru, o, g_sb, w_bf16, g.reshape(1, d))


def _gate_weights(w_r, w_i):
    nb, blk, _ = w_r.shape
    half_blocks = nb // 2

    def halves(w):
        eye = jnp.eye(half_blocks, dtype=w.dtype)
        w = w.reshape(2, half_blocks, blk, blk)
        return jnp.einsum('cnij,nm->cnimj', w, eye).reshape(2, half_blocks * blk, half_blocks * blk)

    return jnp.concatenate([halves(w_r), halves(w_i)], axis=-1).astype(BF16)


def _feature_major(a):
    lead = a.shape[:-3]
    n = len(lead)
    return jnp.transpose(a, (*range(n), n + 1, n + 2, n)).reshape(*lead, a.shape[-2] * a.shape[-1], a.shape[-3])


def _position_major(a, heads):
    lead = a.shape[:-2]
    n = len(lead)
    a = a.reshape(*lead, heads, a.shape[-2] // heads, a.shape[-1])
    return jnp.transpose(a, (*range(n), n + 2, n, n + 1))


def kernel(x_prompt, x_sample, cache_k, cache_v, state_lru_h, state_conv, page_table, norm_pre_g, norm_post_g, w_in, conv_w, conv_b, w_rgate, b_rgate, w_igate, b_igate, lru_lambda, sb_bias, w_out):
    bsz, seq, d_model = x_prompt.shape
    dec_b, dec_s, _ = x_sample.shape
    depth = w_in.shape[0]
    d_lru = lru_lambda.shape[1]
    assert dec_s == 1 and d_lru == D_HEADS
    width = d_lru
    ckt = _feature_major(cache_k)
    cvt = _feature_major(cache_v)
    col = lambda i: slice(i * width, (i + 1) * width)

    xp = x_prompt
    xs = x_sample.reshape(dec_b * dec_s, d_model)
    zero_conv = jnp.zeros((bsz, CONV_W - 1, d_lru), F32)
    zero_h = jnp.zeros((bsz, d_lru), F32)
    kp_l, vp_l, hp_l, cp_l, ks_l, vs_l, hs_l, cs_l = ([] for _ in range(8))
    prev = None
    for l in range(depth):
        w_l = w_in[l].astype(BF16)
        w_ugqg = jnp.concatenate([w_l[:, col(0)], w_l[:, col(1)], w_l[:, col(2)], w_l[:, col(5)]], axis=1)
        w_kv_t = jnp.transpose(w_l[:, 3 * width:5 * width])
        w_out_l = w_out[l].astype(BF16)
        wg = _gate_weights(w_rgate[l], w_igate[l])
        lru_p = (conv_w[l], conv_b[l], wg, b_rgate[l], b_igate[l], lru_lambda[l])

        outs = _inproj_lru(xp, norm_pre_g[l], w_ugqg, w_kv_t, zero_conv, zero_h, *lru_p, prev=prev)
        if prev is not None:
            xp, outs = outs[0], outs[1:]
        g_lru, q, g_sb, kt, vt, h, h_last, buf = outs
        if l == depth - 1 and depth > 1:
            o, k_stack, v_stack = _sb_prompt(q, kt, vt, sb_bias[l], earlier_kv=(kp_l, vp_l))
        else:
            o = _sb_prompt(q, kt, vt, sb_bias[l])
        prev = (h, g_lru, o, g_sb, w_out_l, norm_post_g[l])
        kp_l.append(kt)
        vp_l.append(vt)
        hp_l.append(h_last.reshape(bsz, d_lru))
        cp_l.append(buf)

        u, g_lru, q, k, v, g_sb = _inproj(xs, norm_pre_g[l], w_l, width)
        h, buf_t = _lru_step(u, jnp.swapaxes(state_conv[l], 0, 1), state_lru_h[l], *lru_p)
        o = _sb_paged(q, k, v, ckt, cvt, l, page_table, sb_bias[l])
        xs = _outproj(xs, h, g_lru, o, g_sb, w_out_l, norm_post_g[l])
        ks_l.append(k.reshape(dec_b, dec_s, N_HEADS, HEAD_DIM))
        vs_l.append(v.reshape(dec_b, dec_s, N_HEADS, HEAD_DIM))
        hs_l.append(h)
        cs_l.append(buf_t)

    if depth == 1:
        k_stack, v_stack = kp_l[0][None], vp_l[0][None]
    flat = lambda a: a.reshape(bsz * seq, a.shape[-1])
    xp = _outproj(flat(xp), *(flat(a) for a in prev[:4]), *prev[4:]).reshape(bsz, seq, d_model)
    return (xp, xs.reshape(dec_b, dec_s, d_model),
            _position_major(k_stack, N_HEADS), _position_major(v_stack, N_HEADS),
            jnp.stack(hp_l), jnp.stack(cp_l),
            jnp.stack(ks_l), jnp.stack(vs_l), jnp.stack(hs_l), jnp.swapaxes(jnp.stack(cs_l), 1, 2))
```

```python
import functools

import jax
import jax.numpy as jnp
from jax import lax
from jax.experimental import pallas as pl
from jax.experimental.pallas import tpu as pltpu

F32 = jnp.float32
BF16 = jnp.bfloat16

RMS_EPS = 1e-6
LRU_C = 8.0
CONV_W = 4
N_HEADS = 8
HEAD_DIM = 64
D_HEADS = N_HEADS * HEAD_DIM

LANES = 128
SUBLANES = 8
VMEM_LIMIT_BYTES = 56 * 1024 * 1024

ROW_TILE = 1024
PROMPT_TILE = 512
ATT_TILE = 256
PAGES_PER_STEP = 32
SPLIT_PARTS = 3


def _params(*semantics):
    return pltpu.CompilerParams(dimension_semantics=semantics, vmem_limit_bytes=VMEM_LIMIT_BYTES)


def _softplus(z):
    return jnp.maximum(z, 0.0) + jnp.log1p(jnp.exp(-jnp.abs(z)))


LOG2_E = 1.4426950408889634


def _log2_one_minus_sigmoid(z2):
    n2 = -z2
    return jnp.minimum(n2, 0.0) - jnp.log2(1.0 + jnp.exp2(jnp.minimum(z2, n2)))


def _dot(a, b):
    return jnp.dot(a, b, preferred_element_type=F32)


def _dot_nt(a, b):
    return lax.dot_general(a, b, (((1,), (1,)), ((), ())), preferred_element_type=F32)


def _split_bf16(x, parts):
    pieces, r = [], x
    for _ in range(parts):
        p = r.astype(BF16)
        pieces.append(p)
        r = r - p.astype(F32)
    return pieces


def _dot_f32_by_01(x, w01, parts=SPLIT_PARTS):
    return sum(_dot(p, w01) for p in _split_bf16(x, parts))


def _dot_01_by_f32(w01, x, parts=SPLIT_PARTS):
    return sum(_dot(w01, p) for p in _split_bf16(x, parts))


def _sigmoid(x):
    return 0.5 * (jnp.tanh(0.5 * x) + 1.0)


def _rms_norm_bf16(x, g):
    y = x * lax.rsqrt(jnp.mean(x * x, axis=-1, keepdims=True) + RMS_EPS)
    return (y * g).astype(BF16)


def _inproj_kernel(x_ref, g_ref, w_ref, *out_refs):
    xn = _rms_norm_bf16(x_ref[...], g_ref[...])
    for i, o_ref in enumerate(out_refs):
        width = o_ref.shape[-1]
        o_ref[...] = _dot(xn, w_ref[:, i * width:(i + 1) * width])


def _inproj(x2d, g, w_bf16, width):
    n, d = x2d.shape
    n_out = w_bf16.shape[1] // width
    return pl.pallas_call(
        _inproj_kernel,
        out_shape=[jax.ShapeDtypeStruct((n, width), F32)] * n_out,
        compiler_params=pltpu.CompilerParams(vmem_limit_bytes=VMEM_LIMIT_BYTES),
    )(x2d, g.reshape(1, d), w_bf16)


def _lru_gate_logits(xc, wg_ref):
    half = xc.shape[-1] // 2
    xb = xc.astype(BF16)
    g0 = _dot(xb[:, :half], wg_ref[0])
    g1 = _dot(xb[:, half:], wg_ref[1])
    return (jnp.concatenate([g0[:, :half], g1[:, :half]], axis=-1),
            jnp.concatenate([g0[:, half:], g1[:, half:]], axis=-1))


def _lru_coefficients(xc, r_logit, i_logit, br, bi, lam):
    r = _sigmoid(r_logit + br)
    i = _sigmoid(i_logit + bi)
    log_a = LRU_C * r * (-_softplus(-lam))
    a = jnp.exp(log_a)
    b = jnp.sqrt(-jnp.tanh(log_a) * (a * a + 1.0)) * (i * xc)
    return a, b


def _inproj_lru_kernel(*refs, after_outproj):
    if after_outproj:
        x = _outproj_rows(*refs[:7])
        refs = refs[7:]
    else:
        x = refs[0][...]
        refs = refs[1:]
    g_ref, w_ref, wt_ref, conv0_ref, h0_ref, cw_ref, cb_ref, wg_ref, br_ref, bi_ref, lam_ref = refs[:11]
    refs = refs[11:]
    if after_outproj:
        refs[0][...] = x
        refs = refs[1:]
    gl_ref, q_ref, gs_ref, kt_ref, vt_ref, h_ref, hlast_ref, buf_ref, ext_ref, a_ref, b_ref, hc_ref = refs
    tc, d = h_ref.shape
    pad = SUBLANES
    hist = CONV_W - 1

    @pl.when(pl.program_id(1) == 0)
    def _():
        hc_ref[...] = h0_ref[...]
        ext_ref[pl.ds(pad - hist, hist), :] = conv0_ref[...]

    xn = _rms_norm_bf16(x, g_ref[...])
    ext_ref[pl.ds(pad, tc), :] = _dot(xn, w_ref[:, 0:d])
    for block, o_ref in ((1, gl_ref), (2, q_ref), (5, gs_ref)):
        o_ref[...] = _dot(xn, w_ref[:, block * d:(block + 1) * d])
    for i, o_ref in enumerate((kt_ref, vt_ref)):
        o_ref[...] = _dot_nt(wt_ref[pl.ds(i * d, d), :], xn)

    xc = cb_ref[...] + ext_ref[pl.ds(pad - hist, tc), :] * cw_ref[pl.ds(0, 1), :]
    for j in range(1, CONV_W):
        xc = xc + ext_ref[pl.ds(pad - hist + j, tc), :] * cw_ref[pl.ds(j, 1), :]
    tail = ext_ref[pl.ds(pad + tc - hist, hist), :]
    ext_ref[pl.ds(pad - hist, hist), :] = tail
    buf_ref[...] = tail
    a, b = _lru_coefficients(xc, *_lru_gate_logits(xc, wg_ref), br_ref[...], bi_ref[...], lam_ref[...])

    groups = tc // SUBLANES
    ga = a.reshape(groups, SUBLANES, d)
    gb = b.reshape(groups, SUBLANES, d)
    step_in_group = lax.broadcasted_iota(jnp.int32, ga.shape, 1)
    for k in (1 << e for e in range(SUBLANES.bit_length() - 1)):
        has_prev = step_in_group >= k
        gb = jnp.where(has_prev, ga * pltpu.roll(gb, k, axis=1) + gb, gb)
        ga = jnp.where(has_prev, ga * pltpu.roll(ga, k, axis=1), ga)
    a_ref[...] = ga.reshape(tc, d)
    b_ref[...] = gb.reshape(tc, d)

    def group(i, h):
        rows = pl.ds(pl.multiple_of(i * SUBLANES, SUBLANES), SUBLANES)
        hg = a_ref[rows, :] * h + b_ref[rows, :]
        h_ref[rows, :] = hg
        return hg[SUBLANES - 1:SUBLANES, :]

    h = lax.fori_loop(0, groups, group, hc_ref[...], unroll=4)
    hc_ref[...] = h
    hlast_ref[...] = h


def _inproj_lru(x, g, w_ugqg, w_kv_t, conv0, h0, cw, cb, wg, br, bi, lam, prev=None):
    bsz, t, d = x.shape
    width = lam.shape[-1]
    tm = min(PROMPT_TILE, t)
    hist = CONV_W - 1
    assert t % tm == 0 and tm >= hist and tm % SUBLANES == 0
    full = lambda a: pl.BlockSpec(a.shape, lambda b, i: (0,) * a.ndim)
    vec = lambda n: pl.BlockSpec((1, n), lambda b, i: (0, 0))
    rows = lambda n: pl.BlockSpec((None, tm, n), lambda b, i: (b, i, 0))
    cols = lambda: pl.BlockSpec((None, width, tm), lambda b, i: (b, 0, i))
    per_seq = lambda n: pl.BlockSpec((None, n, width), lambda b, i: (b, 0, 0))
    seq_state = lambda n: jax.ShapeDtypeStruct((bsz, n, width), F32)

    operands, in_specs = [x], [rows(d)]
    out_specs, out_shape = [], []
    if prev is not None:
        h_p, gl_p, o_p, gs_p, w_out, g_post = prev
        operands += [h_p, gl_p, o_p, gs_p, w_out, g_post.reshape(1, d)]
        in_specs += [rows(width)] * 4 + [full(w_out), vec(d)]
        out_specs, out_shape = [rows(d)], [jax.ShapeDtypeStruct((bsz, t, d), F32)]
    operands += [g.reshape(1, d), w_ugqg, w_kv_t, conv0, h0.reshape(bsz, 1, width), cw, cb.reshape(1, width), wg,
                 br.reshape(1, width), bi.reshape(1, width), lam.reshape(1, width)]
    in_specs += [vec(d), full(w_ugqg), full(w_kv_t), per_seq(hist), per_seq(1), full(cw), vec(width), full(wg),
                 vec(width), vec(width), vec(width)]
    out_specs += [rows(width)] * 3 + [cols(), cols(), rows(width), per_seq(1), per_seq(hist)]
    out_shape += [seq_state(t)] * 3 + [jax.ShapeDtypeStruct((bsz, width, t), F32)] * 2 + [
        seq_state(t), seq_state(1), seq_state(hist)]
    return pl.pallas_call(
        functools.partial(_inproj_lru_kernel, after_outproj=prev is not None),
        grid=(bsz, t // tm),
        in_specs=in_specs,
        out_specs=out_specs,
        out_shape=out_shape,
        scratch_shapes=[
            pltpu.VMEM((tm + SUBLANES, width), F32),
            pltpu.VMEM((tm, width), F32),
            pltpu.VMEM((tm, width), F32),
            pltpu.VMEM((1, width), F32),
        ],
        compiler_params=_params("parallel", "arbitrary"),
    )(*operands)


def _lru_step_kernel(u_ref, conv_ref, h0_ref, cw_ref, cb_ref, wg_ref, br_ref, bi_ref, lam_ref,
                     h_ref, buf_ref):
    u = u_ref[...]
    xc = cb_ref[...] + u * cw_ref[pl.ds(CONV_W - 1, 1), :]
    for j in range(CONV_W - 1):
        xc = xc + conv_ref[j] * cw_ref[pl.ds(j, 1), :]
    a, b = _lru_coefficients(xc, *_lru_gate_logits(xc, wg_ref), br_ref[...], bi_ref[...], lam_ref[...])
    h_ref[...] = a * h0_ref[...] + b
    for j in range(CONV_W - 2):
        buf_ref[j] = conv_ref[j + 1]
    buf_ref[CONV_W - 2] = u


def _lru_step(u, conv_t, h0, cw, cb, wg, br, bi, lam):
    bsz, d = u.shape
    return pl.pallas_call(
        _lru_step_kernel,
        out_shape=[jax.ShapeDtypeStruct((bsz, d), F32), jax.ShapeDtypeStruct(conv_t.shape, F32)],
        compiler_params=pltpu.CompilerParams(vmem_limit_bytes=VMEM_LIMIT_BYTES),
    )(u, conv_t, h0, cw, cb.reshape(1, d), wg, br.reshape(1, d), bi.reshape(1, d), lam.reshape(1, d))


def _sb_prompt_kernel(bias_ref, tri_ref, q_ref, kt_ref, vt_ref, *refs, n_stacked):
    tq = q_ref.shape[0]
    tk = tri_ref.shape[0]
    qi = pl.program_id(1)
    if n_stacked:
        earlier, refs = (refs[:n_stacked], refs[n_stacked:2 * n_stacked]), refs[2 * n_stacked:]
        o_ref, k_stack_ref, v_stack_ref, qm_ref, pre_ref, total_ref, carry_ref = refs
        own_cols = pl.ds(pl.multiple_of(qi * tq, tq), tq)
        for stack_ref, earlier_refs, own_ref in ((k_stack_ref, earlier[0], kt_ref), (v_stack_ref, earlier[1], vt_ref)):
            for slot, e_ref in enumerate(earlier_refs):
                stack_ref[slot] = e_ref[...]
            stack_ref[n_stacked] = own_ref[:, own_cols]
    else:
        o_ref, qm_ref, pre_ref, total_ref, carry_ref = refs
    pair = 2 * HEAD_DIM
    scale = HEAD_DIM ** -0.5

    lane_head = lax.broadcasted_iota(jnp.int32, (tq, pair), 1) // HEAD_DIM
    for h in range(N_HEADS):
        qp = q_ref[:, pl.ds((h // 2) * pair, pair)] * scale
        qm_ref[h] = jnp.where(lane_head == h % 2, qp, 0.0).astype(BF16)
    row_head = lax.broadcasted_iota(jnp.int32, (pair, tk), 0) // HEAD_DIM
    visible = (lax.broadcasted_iota(jnp.int32, (tq, tk), 1) < lax.broadcasted_iota(jnp.int32, (tq, tk), 0))

    def keys_of(j):
        return pl.ds(pl.multiple_of(j * tk, tk), tk)

    def front(j, diagonal):
        for p in range(N_HEADS // 2):
            ktp = kt_ref[pl.ds(p * pair, pair), keys_of(j)].astype(BF16)
            for s in range(2):
                h = 2 * p + s
                z = (_dot(qm_ref[h], ktp) + bias_ref[h]) * LOG2_E
                m = _log2_one_minus_sigmoid(z)
                if diagonal:
                    m = jnp.where(visible, m, 0.0)
                csum = _dot(m.astype(BF16), tri_ref[...])
                pre = z + csum
                pre_ref[h] = jnp.where(visible, pre, -jnp.inf) if diagonal else pre
                total_ref[h] = csum[:, 0:1]

    def back(j):
        for p in range(N_HEADS // 2):
            feats = pl.ds(p * pair, pair)
            vtp = vt_ref[feats, keys_of(j)].astype(BF16)
            out = None
            for s in range(2):
                h = 2 * p + s
                carry = carry_ref[h]
                w = jnp.exp2(pre_ref[h] + carry)
                pv = _dot_nt(w.astype(BF16), jnp.where(row_head == s, vtp, jnp.zeros_like(vtp)))
                out = pv if out is None else out + pv
                carry_ref[h] = carry + total_ref[h]
            o_ref[:, feats] += out

    o_ref[...] = jnp.zeros(o_ref.shape, F32)
    carry_ref[...] = jnp.zeros(carry_ref.shape, F32)
    front(qi, True)

    def trip(i, c):
        back(qi - i + 1)
        front(qi - i, False)
        return c

    lax.fori_loop(1, qi + 1, trip, 0)
    back(0)


def _sb_prompt(q, kt, vt, bias, earlier_kv=None):
    bsz, t, d = q.shape
    tq = min(ATT_TILE, t)
    assert t % tq == 0 and d == D_HEADS
    n_stacked = 0 if earlier_kv is None else len(earlier_kv[0])
    tri = (jnp.arange(tq)[:, None] >= jnp.arange(tq)[None, :]).astype(BF16)
    kv_spec = pl.BlockSpec((None, d, t), lambda b, i: (b, 0, 0))
    operands = [bias, tri, q, kt, vt]
    in_specs = [pl.BlockSpec(memory_space=pltpu.SMEM), pl.BlockSpec((tq, tq), lambda b, i: (0, 0)),
                pl.BlockSpec((None, tq, d), lambda b, i: (b, i, 0)), kv_spec, kv_spec]
    out_specs = [pl.BlockSpec((None, tq, d), lambda b, i: (b, i, 0))]
    out_shape = [jax.ShapeDtypeStruct((bsz, t, d), F32)]
    if n_stacked:
        operands += [*earlier_kv[0], *earlier_kv[1]]
        in_specs += [pl.BlockSpec((None, d, tq), lambda b, i: (b, 0, i))] * (2 * n_stacked)
        out_specs += [pl.BlockSpec((n_stacked + 1, None, d, tq), lambda b, i: (0, b, 0, i))] * 2
        out_shape += [jax.ShapeDtypeStruct((n_stacked + 1, bsz, d, t), F32)] * 2
    outs = pl.pallas_call(
        functools.partial(_sb_prompt_kernel, n_stacked=n_stacked),
        grid=(bsz, t // tq),
        in_specs=in_specs,
        out_specs=out_specs,
        out_shape=out_shape,
        scratch_shapes=[pltpu.VMEM((N_HEADS, tq, 2 * HEAD_DIM), BF16), pltpu.VMEM((N_HEADS, tq, tq), F32),
                        pltpu.VMEM((N_HEADS, tq, 1), F32), pltpu.VMEM((N_HEADS, tq, 1), F32)],
        compiler_params=_params("parallel", "arbitrary"),
    )(*operands)
    return outs if n_stacked else outs[0]


def _sb_paged_kernel(pt_ref, q_ref, knew_ref, vnew_ref, bias_ref, *refs, pages_per_step, past_len):
    del pt_ref
    pp = pages_per_step
    k_refs, v_refs = refs[:pp], refs[pp:2 * pp]
    o_ref, qcol_ref, acc_ref, carry_ref, w_ref = refs[2 * pp:]
    j = pl.program_id(1)
    d, page = k_refs[0].shape
    rows = pp * N_HEADS
    scale = HEAD_DIM ** -0.5

    @pl.when(j == 0)
    def _():
        q = q_ref[...]
        qcol_ref[...] = jnp.transpose(jnp.broadcast_to(q, (page, d)))
        assert knew_ref.shape[0] == 1
        new_idx = jnp.zeros((N_HEADS, page), jnp.int32)
        mask = (past_len + new_idx) < (past_len + new_idx)
        own = (lax.broadcasted_iota(jnp.int32, (N_HEADS, d), 0)
               == lax.broadcasted_iota(jnp.int32, (N_HEADS, d), 1) // HEAD_DIM)
        z = jnp.sum(jnp.where(own, q * knew_ref[...], 0.0), axis=-1, keepdims=True) * scale
        head_bias = jnp.concatenate([bias_ref[pl.ds(h * pp, 1), :] for h in range(N_HEADS)], axis=0)
        z = (z + head_bias) * LOG2_E
        carry_ref[...] = jnp.where(mask, _log2_one_minus_sigmoid(z), 0.0)
        w = jnp.where(mask, jnp.exp2(z), 0.0)
        vnew_col = jnp.transpose(jnp.broadcast_to(vnew_ref[...], (page, d)))
        acc_ref[...] = (vnew_col.reshape(N_HEADS, HEAD_DIM, page) * w[:, None, :]).reshape(d, page)

    z_rows = []
    for h in range(N_HEADS):
        feats = pl.ds(h * HEAD_DIM, HEAD_DIM)
        qh = qcol_ref[feats, :]
        z_rows += [jnp.sum(k_refs[i][feats, :] * qh, axis=0, keepdims=True) for i in range(pp)]
    z = jnp.concatenate(z_rows, axis=0)
    z = (z * scale + bias_ref[...]) * LOG2_E
    m = _log2_one_minus_sigmoid(z)
    ki = lax.broadcasted_iota(jnp.int32, (page, page), 0)
    si = lax.broadcasted_iota(jnp.int32, (page, page), 1)
    tri = (ki >= si).astype(BF16)
    ri = lax.broadcasted_iota(jnp.int32, (rows, rows), 0)
    ci = lax.broadcasted_iota(jnp.int32, (rows, rows), 1)
    later = ((ri // pp == ci // pp) & (ci > ri)).astype(BF16)
    csum = _dot_f32_by_01(m, tri)
    total = jnp.broadcast_to(csum[:, 0:1], csum.shape)
    after = _dot_01_by_f32(later, total)
    carry = carry_ref[...]
    carry_rows = jnp.concatenate([jnp.broadcast_to(carry[h:h + 1, :], (pp, page)) for h in range(N_HEADS)], axis=0)
    w_ref[...] = jnp.exp2(z + csum + after + carry_rows)
    for h in range(N_HEADS):
        feats = pl.ds(h * HEAD_DIM, HEAD_DIM)
        acc = acc_ref[feats, :]
        for i in range(pp):
            acc = acc + v_refs[i][feats, :] * w_ref[pl.ds(h * pp + i, 1), :]
        acc_ref[feats, :] = acc
    first = after + total
    carry_ref[...] = carry + jnp.concatenate([first[h * pp:h * pp + 1, :] for h in range(N_HEADS)], axis=0)

    @pl.when(j == pl.num_programs(1) - 1)
    def _():
        o_ref[...] = jnp.sum(jnp.transpose(acc_ref[...]), axis=0, keepdims=True)


def _sb_paged(q, k_new, v_new, cache_kt, cache_vt, layer, page_table, bias):
    bsz, d = q.shape
    n_pages = page_table.shape[1]
    page = cache_kt.shape[3]
    assert d == D_HEADS and page == LANES
    pp = min(PAGES_PER_STEP, n_pages)
    assert n_pages % pp == 0
    steps = n_pages // pp
    bias_tile = jnp.broadcast_to(bias.reshape(N_HEADS, 1, 1), (N_HEADS, pp, page)).reshape(N_HEADS * pp, page)

    def page_spec(i):
        return pl.BlockSpec((None, None, d, page),
                            lambda b, j, pt: (layer, pt[b, n_pages - pp * (j + 1) + i], 0, 0))

    seq = lambda: pl.BlockSpec((None, 1, d), lambda b, j, pt: (b, 0, 0))
    kernel = functools.partial(_sb_paged_kernel, pages_per_step=pp, past_len=n_pages * page)
    out = pl.pallas_call(
        kernel,
        grid_spec=pltpu.PrefetchScalarGridSpec(
            num_scalar_prefetch=1,
            grid=(bsz, steps),
            in_specs=[seq(), seq(), seq(), pl.BlockSpec(bias_tile.shape, lambda b, j, pt: (0, 0))]
            + [page_spec(i) for i in range(pp)] * 2,
            out_specs=seq(),
            scratch_shapes=[pltpu.VMEM((d, page), F32), pltpu.VMEM((d, page), F32),
                            pltpu.VMEM((N_HEADS, page), F32), pltpu.VMEM((N_HEADS * pp, page), F32)],
        ),
        out_shape=jax.ShapeDtypeStruct((bsz, 1, d), F32),
        compiler_params=_params("parallel", "arbitrary"),
    )(page_table, q.reshape(bsz, 1, d), k_new.reshape(bsz, 1, d), v_new.reshape(bsz, 1, d), bias_tile,
      *([cache_kt] * pp), *([cache_vt] * pp))
    return out.reshape(bsz, d)


def _outproj_rows(x_ref, h_ref, gl_ref, o_ref, gs_ref, w_ref, g_ref):
    d_lru = h_ref.shape[-1]
    y_lru = (h_ref[...] * jax.nn.silu(gl_ref[...])).astype(BF16)
    y_sb = (o_ref[...] * jax.nn.silu(gs_ref[...])).astype(BF16)
    m = _dot(y_lru, w_ref[pl.ds(0, d_lru), :]) + _dot(y_sb, w_ref[pl.ds(d_lru, w_ref.shape[0] - d_lru), :])
    n = m * lax.rsqrt(jnp.mean(m * m, axis=-1, keepdims=True) + RMS_EPS)
    return x_ref[...] + n * g_ref[...]


def _outproj_kernel(x_ref, h_ref, gl_ref, o_ref, gs_ref, w_ref, g_ref, y_ref):
    y_ref[...] = _outproj_rows(x_ref, h_ref, gl_ref, o_ref, gs_ref, w_ref, g_ref)


def _outproj(x2d, h, g_lru, o, g_sb, w_bf16, g):
    n, d = x2d.shape
    tm = min(ROW_TILE, n)
    assert n % tm == 0
    rows = lambda w: pl.BlockSpec((tm, w), lambda i: (i, 0))
    return pl.pallas_call(
        _outproj_kernel,
        grid=(n // tm,),
        in_specs=[rows(d), rows(h.shape[1]), rows(g_lru.shape[1]), rows(o.shape[1]), rows(g_sb.shape[1]),
                  pl.BlockSpec(w_bf16.shape, lambda i: (0, 0)),
                  pl.BlockSpec((1, d), lambda i: (0, 0))],
        out_specs=rows(d),
        out_shape=jax.ShapeDtypeStruct((n, d), F32),
        compiler_params=_params("parallel"),
    )(x2d, h, g_lru, o, g_sb, w_bf16, g.reshape(1, d))


def _gate_weights(w_r, w_i):
    nb, blk, _ = w_r.shape
    half_blocks = nb // 2

    def halves(w):
        eye = jnp.eye(half_blocks, dtype=w.dtype)
        w = w.reshape(2, half_blocks, blk, blk)
        return jnp.einsum('cnij,nm->cnimj', w, eye).reshape(2, half_blocks * blk, half_blocks * blk)

    return jnp.concatenate([halves(w_r), halves(w_i)], axis=-1).astype(BF16)


def _feature_major(a):
    lead = a.shape[:-3]
    n = len(lead)
    return jnp.transpose(a, (*range(n), n + 1, n + 2, n)).reshape(*lead, a.shape[-2] * a.shape[-1], a.shape[-3])


def _position_major(a, heads):
    lead = a.shape[:-2]
    n = len(lead)
    a = a.reshape(*lead, heads, a.shape[-2] // heads, a.shape[-1])
    return jnp.transpose(a, (*range(n), n + 2, n, n + 1))


def kernel(x_prompt, x_sample, cache_k, cache_v, state_lru_h, state_conv, page_table, norm_pre_g, norm_post_g, w_in, conv_w, conv_b, w_rgate, b_rgate, w_igate, b_igate, lru_lambda, sb_bias, w_out):
    bsz, seq, d_model = x_prompt.shape
    dec_b, dec_s, _ = x_sample.shape
    depth = w_in.shape[0]
    d_lru = lru_lambda.shape[1]
    assert dec_s == 1 and d_lru == D_HEADS
    width = d_lru
    ckt = _feature_major(cache_k)
    cvt = _feature_major(cache_v)
    col = lambda i: slice(i * width, (i + 1) * width)

    xp = x_prompt
    xs = x_sample.reshape(dec_b * dec_s, d_model)
    zero_conv = jnp.zeros((bsz, CONV_W - 1, d_lru), F32)
    zero_h = jnp.zeros((bsz, d_lru), F32)
    kp_l, vp_l, hp_l, cp_l, ks_l, vs_l, hs_l, cs_l = ([] for _ in range(8))
    prev = None
    for l in range(depth):
        w_l = w_in[l].astype(BF16)
        w_kv_t = jnp.transpose(w_l[:, 3 * width:5 * width])
        w_out_l = w_out[l].astype(BF16)
        wg = _gate_weights(w_rgate[l], w_igate[l])
        lru_p = (conv_w[l], conv_b[l], wg, b_rgate[l], b_igate[l], lru_lambda[l])

        outs = _inproj_lru(xp, norm_pre_g[l], w_l, w_kv_t, zero_conv, zero_h, *lru_p, prev=prev)
        if prev is not None:
            xp, outs = outs[0], outs[1:]
        g_lru, q, g_sb, kt, vt, h, h_last, buf = outs
        if l == depth - 1 and depth > 1:
            o, k_stack, v_stack = _sb_prompt(q, kt, vt, sb_bias[l], earlier_kv=(kp_l, vp_l))
        else:
            o = _sb_prompt(q, kt, vt, sb_bias[l])
        prev = (h, g_lru, o, g_sb, w_out_l, norm_post_g[l])
        kp_l.append(kt)
        vp_l.append(vt)
        hp_l.append(h_last.reshape(bsz, d_lru))
        cp_l.append(buf)

        u, g_lru, q, k, v, g_sb = _inproj(xs, norm_pre_g[l], w_l, width)
        h, buf_t = _lru_step(u, jnp.swapaxes(state_conv[l], 0, 1), state_lru_h[l], *lru_p)
        o = _sb_paged(q, k, v, ckt, cvt, l, page_table, sb_bias[l])
        xs = _outproj(xs, h, g_lru, o, g_sb, w_out_l, norm_post_g[l])
        ks_l.append(k.reshape(dec_b, dec_s, N_HEADS, HEAD_DIM))
        vs_l.append(v.reshape(dec_b, dec_s, N_HEADS, HEAD_DIM))
        hs_l.append(h)
        cs_l.append(buf_t)

    if depth == 1:
        k_stack, v_stack = kp_l[0][None], vp_l[0][None]
    flat = lambda a: a.reshape(bsz * seq, a.shape[-1])
    xp = _outproj(flat(xp), *(flat(a) for a in prev[:4]), *prev[4:]).reshape(bsz, seq, d_model)
    return (xp, xs.reshape(dec_b, dec_s, d_model),
            _position_major(k_stack, N_HEADS), _position_major(v_stack, N_HEADS),
            jnp.stack(hp_l), jnp.stack(cp_l),
            jnp.stack(ks_l), jnp.stack(vs_l), jnp.stack(hs_l), jnp.swapaxes(jnp.stack(cs_l), 1, 2))
```
